```python
import math
import jax, jax.numpy as jnp
from jax import lax
import numpy as np

D_MODEL = 2048
BATCH = 1
SEQ = 8192
DEPTH = 4

D_MIX = D_MODEL
ATTN_WIDTH = D_MIX // 2
CONV_WIDTH = D_MIX - ATTN_WIDTH
DIFF_HEAD_DIM = 64
V_HEAD_DIM = 2 * DIFF_HEAD_DIM
N_DIFF_HEADS = ATTN_WIDTH // V_HEAD_DIM
CONV_GROUP = 128
N_CONV_GROUPS = CONV_WIDTH // CONV_GROUP
ROT_DIM = DIFF_HEAD_DIM // 4
ROPE_THETA = 500000.0
CONV_K = 3
D_FF = 5632
BLOCK_Q = 128
EPS = 1e-5
IN_WIDTH = 3 * ATTN_WIDTH + 3 * CONV_WIDTH
NEG_INF = -1e30

kernel_name = 'hymba_diffattn_shortconv_macaron'


def rms_norm(x, g):
    xf = x.astype(jnp.float32)
    y = xf * lax.rsqrt(jnp.mean(xf * xf, axis=-1, keepdims=True) + EPS)
    return (y * g.astype(jnp.float32)).astype(x.dtype)


def swiglu(h, w_gate, w_up, w_down):
    return (jax.nn.silu(h @ w_gate) * (h @ w_up)) @ w_down


def lambda_init_fn(layer_idx):
    return 0.8 - 0.6 * math.exp(-0.3 * layer_idx)


def rope_tables(seq):
    pos = jnp.arange(seq, dtype=jnp.float32)
    inv_freq = ROPE_THETA ** (-jnp.arange(0, ROT_DIM, 2, dtype=jnp.float32) / ROT_DIM)
    ang = pos[:, None] * inv_freq[None, :]
    return jnp.cos(ang), jnp.sin(ang)


def apply_partial_rope(t, cos, sin):
    rot, rest = t[..., :ROT_DIM], t[..., ROT_DIM:]
    x1, x2 = rot[..., :ROT_DIM // 2], rot[..., ROT_DIM // 2:]
    c = cos[:, None, None, :].astype(t.dtype)
    s = sin[:, None, None, :].astype(t.dtype)
    return jnp.concatenate([x1 * c - x2 * s, x2 * c + x1 * s, rest], axis=-1)


def diff_attention(q1, q2, k1, k2, v, lam):
    b, s, h, d = q1.shape
    nb = s // BLOCK_Q
    scale = d ** -0.5
    kpos = jnp.arange(s)

    def to_blocks(t):
        return jnp.moveaxis(t.reshape(b, nb, BLOCK_Q, h, t.shape[-1]), 1, 0)

    def one_block(args):
        q1b, q2b, start = args
        qpos = start + jnp.arange(BLOCK_Q)
        mask = kpos[None, :] <= qpos[:, None]

        def probs(qb, k):
            sc = jnp.einsum('bqhd,bkhd->bhqk', qb, k).astype(jnp.float32) * scale
            return jax.nn.softmax(jnp.where(mask, sc, NEG_INF), axis=-1)

        attn = probs(q1b, k1) - lam * probs(q2b, k2)
        return jnp.einsum('bhqk,bkhe->bqhe', attn.astype(v.dtype), v)

    starts = jnp.arange(nb) * BLOCK_Q
    out = lax.map(one_block, (to_blocks(q1), to_blocks(q2), starts))
    return jnp.moveaxis(out, 0, 1).reshape(b, s, h, v.shape[-1])


def causal_depthwise_conv(u, w):
    return lax.conv_general_dilated(
        u, w[:, None, :].astype(u.dtype), window_strides=(1,),
        padding=[(CONV_K - 1, 0)], dimension_numbers=('NWC', 'WIO', 'NWC'),
        feature_group_count=u.shape[-1])


def hybrid_mixer(h, w_in, conv_w, lq1, lk1, lq2, lk2, subln_w, w_out, lambda_init, cos, sin):
    b, s, _ = h.shape
    proj = h @ w_in
    q, k, v, gate_b, gate_c, u = jnp.split(
        proj, [ATTN_WIDTH, 2 * ATTN_WIDTH, 3 * ATTN_WIDTH,
               3 * ATTN_WIDTH + CONV_WIDTH, 3 * ATTN_WIDTH + 2 * CONV_WIDTH], axis=-1)
    q = apply_partial_rope(q.reshape(b, s, N_DIFF_HEADS, 2, DIFF_HEAD_DIM), cos, sin)
    k = apply_partial_rope(k.reshape(b, s, N_DIFF_HEADS, 2, DIFF_HEAD_DIM), cos, sin)
    v = v.reshape(b, s, N_DIFF_HEADS, V_HEAD_DIM)
    f32 = jnp.float32
    lam = (jnp.exp(jnp.sum(lq1.astype(f32) * lk1.astype(f32)))
           - jnp.exp(jnp.sum(lq2.astype(f32) * lk2.astype(f32))) + lambda_init)
    attn = diff_attention(q[..., 0, :], q[..., 1, :], k[..., 0, :], k[..., 1, :], v, lam)
    attn = (rms_norm(attn, subln_w) * (1.0 - lambda_init)).reshape(b, s, ATTN_WIDTH)
    conv_out = gate_b * causal_depthwise_conv(gate_c * u, conv_w)
    return jnp.concatenate([attn, conv_out], axis=-1) @ w_out


def setup_inputs(seed: int = 0) -> dict:
    key = jax.random.key(seed)
    ks = jax.random.split(key, 20)
    f32 = jnp.float32

    def nrm(k, shape, scale):
        return jax.random.normal(k, shape, f32) * scale

    def gain(k, shape):
        return 1.0 + 0.02 * jax.random.normal(k, shape, f32)

    return {
        'x': nrm(ks[0], (BATCH, SEQ, D_MODEL), 1.0),
        'norm_ffn1': gain(ks[1], (DEPTH, D_MODEL)),
        'ffn1_w_gate': nrm(ks[2], (DEPTH, D_MODEL, D_FF), D_MODEL ** -0.5),
        'ffn1_w_up': nrm(ks[3], (DEPTH, D_MODEL, D_FF), D_MODEL ** -0.5),
        'ffn1_w_down': nrm(ks[4], (DEPTH, D_FF, D_MODEL), D_FF ** -0.5),
        'norm_mix': gain(ks[5], (DEPTH, D_MODEL)),
        'w_in': nrm(ks[6], (DEPTH, D_MODEL, IN_WIDTH), D_MODEL ** -0.5),
        'conv_w': nrm(ks[7], (DEPTH, CONV_K, CONV_WIDTH), CONV_K ** -0.5),
        'lambda_q1': nrm(ks[8], (DEPTH, DIFF_HEAD_DIM), 0.1),
        'lambda_k1': nrm(ks[9], (DEPTH, DIFF_HEAD_DIM), 0.1),
        'lambda_q2': nrm(ks[10], (DEPTH, DIFF_HEAD_DIM), 0.1),
        'lambda_k2': nrm(ks[11], (DEPTH, DIFF_HEAD_DIM), 0.1),
        'subln_w': gain(ks[12], (DEPTH, V_HEAD_DIM)),
        'w_out': nrm(ks[13], (DEPTH, D_MIX, D_MODEL), D_MIX ** -0.5),
        'norm_ffn2': gain(ks[14], (DEPTH, D_MODEL)),
        'ffn2_w_gate': nrm(ks[15], (DEPTH, D_MODEL, D_FF), D_MODEL ** -0.5),
        'ffn2_w_up': nrm(ks[16], (DEPTH, D_MODEL, D_FF), D_MODEL ** -0.5),
        'ffn2_w_down': nrm(ks[17], (DEPTH, D_FF, D_MODEL), D_FF ** -0.5),
        'norm_final': gain(ks[18], (D_MODEL,)),
    }


def reference(x, norm_ffn1, ffn1_w_gate, ffn1_w_up, ffn1_w_down, norm_mix, w_in, conv_w,
              lambda_q1, lambda_k1, lambda_q2, lambda_k2, subln_w, w_out,
              norm_ffn2, ffn2_w_gate, ffn2_w_up, ffn2_w_down, norm_final):
    cos, sin = rope_tables(x.shape[1])
    for l in range(DEPTH):
        lambda_init = lambda_init_fn(l)
        x = x + 0.5 * swiglu(rms_norm(x, norm_ffn1[l]), ffn1_w_gate[l], ffn1_w_up[l], ffn1_w_down[l])
        x = x + hybrid_mixer(rms_norm(x, norm_mix[l]), w_in[l], conv_w[l],
                             lambda_q1[l], lambda_k1[l], lambda_q2[l], lambda_k2[l],
                             subln_w[l], w_out[l], lambda_init, cos, sin)
        x = x + 0.5 * swiglu(rms_norm(x, norm_ffn2[l]), ffn2_w_gate[l], ffn2_w_up[l], ffn2_w_down[l])
    return rms_norm(x, norm_final)
```

```python
import functools
import math

import jax
import jax.numpy as jnp
from jax import lax
from jax.experimental import pallas as pl
from jax.experimental.pallas import tpu as pltpu

F32 = jnp.float32
BF16 = jnp.bfloat16

EPS = 1e-5
NEG_INF = -1e30
ROPE_THETA = 500000.0
DIFF_HEAD_DIM = 64
HEAD_W = 2 * DIFF_HEAD_DIM
ROT_DIM = DIFF_HEAD_DIM // 4
CONV_K = 3
LOG2E = math.log2(math.e)

LANES = 128
SUBLANES = 8
VMEM_LIMIT = 56 * 1024 * 1024


def _params(*sem):
    return pltpu.CompilerParams(dimension_semantics=sem, vmem_limit_bytes=VMEM_LIMIT)


def _rms(x, g):
    ms = jnp.mean(x * x, axis=-1, keepdims=True)
    return x * lax.rsqrt(ms + EPS) * g


def _ffn_kernel(*refs, final_norm):
    if final_norm:
        x_ref, g_ref, wg_ref, wu_ref, wd_ref, gf_ref, o_ref, h_ref = refs
    else:
        x_ref, g_ref, wg_ref, wu_ref, wd_ref, o_ref, h_ref = refs
    f = pl.program_id(1)

    @pl.when(f == 0)
    def _():
        h_ref[...] = _rms(x_ref[...], g_ref[...]).astype(BF16)
        o_ref[...] = jnp.zeros_like(o_ref)

    h = h_ref[...]
    gate = jnp.dot(h, wg_ref[...], preferred_element_type=F32)
    up = jnp.dot(h, wu_ref[...], preferred_element_type=F32)
    act = (gate * (1.0 / (1.0 + jnp.exp(-gate))) * up).astype(BF16)
    o_ref[...] += jnp.dot(act, wd_ref[...], preferred_element_type=F32)

    @pl.when(f == pl.num_programs(1) - 1)
    def _():
        y = x_ref[...] + 0.5 * o_ref[...]
        if final_norm:
            y = _rms(y, gf_ref[...])
        o_ref[...] = y


def _ffn(x, g, wg, wu, wd, g_final=None, *, tm=512, tf=512):
    s, d = x.shape
    dff = wg.shape[1]
    tm, tf = min(tm, s), min(tf, dff)
    assert s % tm == 0 and dff % tf == 0
    final_norm = g_final is not None
    in_specs = [
        pl.BlockSpec((tm, d), lambda i, f: (i, 0)),
        pl.BlockSpec((1, d), lambda i, f: (0, 0)),
        pl.BlockSpec((d, tf), lambda i, f: (0, f)),
        pl.BlockSpec((d, tf), lambda i, f: (0, f)),
        pl.BlockSpec((tf, d), lambda i, f: (f, 0)),
    ]
    args = [x, g.reshape(1, d), wg, wu, wd]
    if final_norm:
        in_specs.append(pl.BlockSpec((1, d), lambda i, f: (0, 0)))
        args.append(g_final.reshape(1, d))
    return pl.pallas_call(
        functools.partial(_ffn_kernel, final_norm=final_norm),
        grid=(s // tm, dff // tf),
        in_specs=in_specs,
        out_specs=pl.BlockSpec((tm, d), lambda i, f: (i, 0)),
        out_shape=jax.ShapeDtypeStruct((s, d), F32),
        scratch_shapes=[pltpu.VMEM((tm, d), BF16)],
        compiler_params=_params("parallel", "arbitrary"),
        name="ffn_final" if final_norm else "ffn",
    )(*args)


def _qkv_kernel(x_ref, g_ref, w_ref, c_ref, s1_ref, s2_ref, o_ref, hn_ref, *, q_scale):
    j = pl.program_id(1)

    @pl.when(j == 0)
    def _():
        hn_ref[...] = _rms(x_ref[...], g_ref[...]).astype(BF16)

    acc = jnp.dot(hn_ref[...], w_ref[...], preferred_element_type=F32)
    n_chunks = acc.shape[1] // LANES

    @pl.when(j < 2)
    def _():
        scale = jnp.where(j == 0, q_scale, 1.0).astype(F32)
        cos, sin_up, sin_dn = c_ref[...], s1_ref[...], s2_ref[...]
        for c in range(n_chunks):
            t = acc[:, c * LANES:(c + 1) * LANES]
            up = pltpu.roll(t, LANES - ROT_DIM // 2, axis=1)
            dn = pltpu.roll(t, ROT_DIM // 2, axis=1)
            r = t * cos + up * sin_up + dn * sin_dn
            o_ref[:, c * LANES:(c + 1) * LANES] = (r * scale).astype(BF16)

    @pl.when(j == 2)
    def _():
        o_ref[...] = acc.astype(BF16)


def _qkv(x, g, w_in, rope_c, rope_s1, rope_s2, attn_w, *, tm=512):
    s, d = x.shape
    tm = min(tm, s)
    assert s % tm == 0 and attn_w % LANES == 0
    q_scale = DIFF_HEAD_DIM ** -0.5 * LOG2E
    tab = pl.BlockSpec((tm, LANES), lambda i, j: (i, 0))
    return pl.pallas_call(
        functools.partial(_qkv_kernel, q_scale=q_scale),
        grid=(s // tm, 3),
        in_specs=[
            pl.BlockSpec((tm, d), lambda i, j: (i, 0)),
            pl.BlockSpec((1, d), lambda i, j: (0, 0)),
            pl.BlockSpec((d, attn_w), lambda i, j: (0, j)),
            tab, tab, tab,
        ],
        out_specs=[
            pl.BlockSpec((tm, attn_w), lambda i, j: (i, j)),
            pl.BlockSpec((tm, d), lambda i, j: (i, 0)),
        ],
        out_shape=[
            jax.ShapeDtypeStruct((s, 3 * attn_w), BF16),
            jax.ShapeDtypeStruct((s, d), BF16),
        ],
        compiler_params=_params("parallel", "arbitrary"),
        name="qkv_rope",
    )(x, g.reshape(1, d), w_in, rope_c, rope_s1, rope_s2)


def _conv_kernel(hn_ref, wb_ref, wc_ref, wu_ref, cw_ref, o_ref, z_ref):
    i = pl.program_id(1)
    tm = hn_ref.shape[0]

    @pl.when(i == 0)
    def _():
        z_ref[0:SUBLANES, :] = jnp.zeros((SUBLANES, z_ref.shape[1]), F32)

    h = hn_ref[...]
    gate_b = jnp.dot(h, wb_ref[...], preferred_element_type=F32)
    gate_c = jnp.dot(h, wc_ref[...], preferred_element_type=F32)
    u = jnp.dot(h, wu_ref[...], preferred_element_type=F32)
    z = gate_c * u
    z_ref[SUBLANES:SUBLANES + tm, :] = z
    z1 = z_ref[SUBLANES - 1:SUBLANES - 1 + tm, :]
    z2 = z_ref[SUBLANES - 2:SUBLANES - 2 + tm, :]
    cw = cw_ref[...]
    y = cw[0:1, :] * z2 + cw[1:2, :] * z1 + cw[2:3, :] * z
    o_ref[...] = (gate_b * y).astype(BF16)
    z_ref[0:SUBLANES, :] = z_ref[tm:tm + SUBLANES, :]


def _conv_branch(hn, w_in, conv_w, attn_w, *, tm=512, tc=512):
    s, d = hn.shape
    cw_width = conv_w.shape[1]
    tm, tc = min(tm, s), min(tc, cw_width)
    assert s % tm == 0 and cw_width % tc == 0 and (3 * attn_w) % tc == 0
    nj = cw_width // tc
    off = 3 * attn_w // tc

    def wspec(k):
        return pl.BlockSpec((d, tc), lambda j, i: (0, off + k * nj + j))

    return pl.pallas_call(
        _conv_kernel,
        grid=(nj, s // tm),
        in_specs=[
            pl.BlockSpec((tm, d), lambda j, i: (i, 0)),
            wspec(0), wspec(1), wspec(2),
            pl.BlockSpec((CONV_K, tc), lambda j, i: (0, j)),
        ],
        out_specs=pl.BlockSpec((tm, tc), lambda j, i: (i, j)),
        out_shape=jax.ShapeDtypeStruct((s, cw_width), BF16),
        scratch_shapes=[pltpu.VMEM((tm + SUBLANES, tc), F32)],
        compiler_params=_params("parallel", "arbitrary"),
        name="gated_conv",
    )(hn, w_in, w_in, w_in, conv_w)


def _attn_kernel(li_ref, q_ref, k_ref, v_ref, lq1_ref, lk1_ref, lq2_ref, lk2_ref, w_ref,
                 o_ref, qs_ref, m_ref, l_ref, acc_ref, *, tk):
    i = pl.program_id(1)
    tq = q_ref.shape[0]

    q = q_ref[...]
    lane = lax.broadcasted_iota(jnp.int32, q.shape, 1)
    zero = jnp.zeros_like(q)
    qs_ref[0:tq, :] = jnp.where(lane < DIFF_HEAD_DIM, q, zero)
    qs_ref[tq:2 * tq, :] = jnp.where(lane >= DIFF_HEAD_DIM, q, zero)
    m_ref[...] = jnp.full(m_ref.shape, NEG_INF, F32)
    l_ref[...] = jnp.zeros(l_ref.shape, F32)
    acc_ref[...] = jnp.zeros(acc_ref.shape, F32)

    def step(kv_start, masked):
        k = k_ref[pl.ds(kv_start, tk), :]
        v = v_ref[pl.ds(kv_start, tk), :]
        s = lax.dot_general(qs_ref[...], k, (((1,), (1,)), ((), ())),
                            preferred_element_type=F32)
        if masked:
            row = lax.broadcasted_iota(jnp.int32, (tq, tk), 0)
            col = lax.broadcasted_iota(jnp.int32, (tq, tk), 1)
            keep = col <= row
            keep = jnp.concatenate([keep, keep], axis=0)
            s = jnp.where(keep, s, NEG_INF)
        m_prev = m_ref[...]
        m_new = jnp.maximum(m_prev, jnp.max(s, axis=-1, keepdims=True))
        alpha = jnp.exp2(m_prev - m_new)
        p = jnp.exp2(s - m_new)
        l_ref[...] = alpha * l_ref[...] + jnp.sum(p, axis=-1, keepdims=True)
        acc_ref[...] = alpha * acc_ref[...] + jnp.dot(
            p.astype(BF16), v, preferred_element_type=F32)
        m_ref[...] = m_new

    def body(j, carry):
        step(pl.multiple_of(j * tk, tk), False)
        return carry

    lax.fori_loop(0, i, body, 0)
    step(pl.multiple_of(i * tq, tq), True)

    lam = (jnp.exp(jnp.sum(lq1_ref[...] * lk1_ref[...], axis=-1, keepdims=True))
           - jnp.exp(jnp.sum(lq2_ref[...] * lk2_ref[...], axis=-1, keepdims=True))
           + li_ref[0])
    o1 = acc_ref[0:tq, :] * (1.0 / l_ref[0:tq, :])
    o2 = acc_ref[tq:2 * tq, :] * (1.0 / l_ref[tq:2 * tq, :])
    o = o1 - lam * o2
    o = _rms(o, w_ref[...]) * (1.0 - li_ref[0])
    o_ref[...] = o.astype(BF16)


def _attention(qkv, lam_init, lq1, lk1, lq2, lk2, subln_w, n_heads, *, tq=512):
    s = qkv.shape[0]
    tq = min(tq, s)
    tk = tq
    assert s % tq == 0
    dh = DIFF_HEAD_DIM
    lspec = pl.BlockSpec((1, dh), lambda h, i: (0, 0))
    return pl.pallas_call(
        functools.partial(_attn_kernel, tk=tk),
        grid=(n_heads, s // tq),
        in_specs=[
            pl.BlockSpec(memory_space=pltpu.SMEM),
            pl.BlockSpec((tq, HEAD_W), lambda h, i: (i, h)),
            pl.BlockSpec((s, HEAD_W), lambda h, i: (0, n_heads + h)),
            pl.BlockSpec((s, HEAD_W), lambda h, i: (0, 2 * n_heads + h)),
            lspec, lspec, lspec, lspec,
            pl.BlockSpec((1, HEAD_W), lambda h, i: (0, 0)),
        ],
        out_specs=pl.BlockSpec((tq, HEAD_W), lambda h, i: (i, h)),
        out_shape=jax.ShapeDtypeStruct((s, n_heads * HEAD_W), BF16),
        scratch_shapes=[
            pltpu.VMEM((2 * tq, HEAD_W), BF16),
            pltpu.VMEM((2 * tq, 1), F32),
            pltpu.VMEM((2 * tq, 1), F32),
            pltpu.VMEM((2 * tq, HEAD_W), F32),
        ],
        compiler_params=_params("parallel", "arbitrary"),
        name="diff_attn",
    )(lam_init, qkv, qkv, qkv, lq1.reshape(1, dh), lk1.reshape(1, dh),
      lq2.reshape(1, dh), lk2.reshape(1, dh), subln_w.reshape(1, HEAD_W))


def _out_kernel(x_ref, a_ref, c_ref, wa_ref, wc_ref, o_ref):
    o_ref[...] = (x_ref[...]
                  + jnp.dot(a_ref[...], wa_ref[...], preferred_element_type=F32)
                  + jnp.dot(c_ref[...], wc_ref[...], preferred_element_type=F32))


def _out_proj(x, attn, conv, w_out, *, tm=512):
    s, d = x.shape
    aw, cw = attn.shape[1], conv.shape[1]
    tm = min(tm, s)
    assert s % tm == 0 and aw == cw
    return pl.pallas_call(
        _out_kernel,
        grid=(s // tm,),
        in_specs=[
            pl.BlockSpec((tm, d), lambda i: (i, 0)),
            pl.BlockSpec((tm, aw), lambda i: (i, 0)),
            pl.BlockSpec((tm, cw), lambda i: (i, 0)),
            pl.BlockSpec((aw, d), lambda i: (0, 0)),
            pl.BlockSpec((cw, d), lambda i: (1, 0)),
        ],
        out_specs=pl.BlockSpec((tm, d), lambda i: (i, 0)),
        out_shape=jax.ShapeDtypeStruct((s, d), F32),
        compiler_params=_params("parallel"),
        name="out_proj",
    )(x, attn, conv, w_out, w_out)


def _rope_lane_tables(seq):
    half = ROT_DIM // 2
    pos = jnp.arange(seq, dtype=F32)
    inv_freq = ROPE_THETA ** (-jnp.arange(0, ROT_DIM, 2, dtype=F32) / ROT_DIM)
    ang = pos[:, None] * inv_freq[None, :]
    cos, sin = jnp.cos(ang), jnp.sin(ang)
    rest = DIFF_HEAD_DIM - ROT_DIM
    ones = jnp.ones((seq, rest), F32)
    c = jnp.concatenate([cos, cos, ones], axis=-1)
    s_up = jnp.concatenate([-sin, jnp.zeros((seq, half + rest), F32)], axis=-1)
    s_dn = jnp.concatenate([jnp.zeros((seq, half), F32), sin, jnp.zeros((seq, rest), F32)], axis=-1)
    rep = LANES // DIFF_HEAD_DIM
    return jnp.tile(c, (1, rep)), jnp.tile(s_up, (1, rep)), jnp.tile(s_dn, (1, rep))


def kernel(x, norm_ffn1, ffn1_w_gate, ffn1_w_up, ffn1_w_down, norm_mix, w_in, conv_w,
           lambda_q1, lambda_k1, lambda_q2, lambda_k2, subln_w, w_out,
           norm_ffn2, ffn2_w_gate, ffn2_w_up, ffn2_w_down, norm_final):
    b, s, d = x.shape
    depth = norm_ffn1.shape[0]
    attn_w = d // 2
    n_heads = attn_w // HEAD_W
    bf = lambda w: w.astype(BF16)
    w1g, w1u, w1d = bf(ffn1_w_gate), bf(ffn1_w_up), bf(ffn1_w_down)
    w2g, w2u, w2d = bf(ffn2_w_gate), bf(ffn2_w_up), bf(ffn2_w_down)
    w_in_b, w_out_b = bf(w_in), bf(w_out)
    rope_c, rope_s1, rope_s2 = _rope_lane_tables(s)

    outs = []
    for bi in range(b):
        xb = x[bi]
        for l in range(depth):
            lam_init = jnp.full((1,), 0.8 - 0.6 * math.exp(-0.3 * l), F32)
            xb = _ffn(xb, norm_ffn1[l], w1g[l], w1u[l], w1d[l])
            qkv, hn = _qkv(xb, norm_mix[l], w_in_b[l], rope_c, rope_s1, rope_s2, attn_w)
            conv = _conv_branch(hn, w_in_b[l], conv_w[l], attn_w)
            attn = _attention(qkv, lam_init, lambda_q1[l], lambda_k1[l], lambda_q2[l],
                              lambda_k2[l], subln_w[l], n_heads)
            xb = _out_proj(xb, attn, conv, w_out_b[l])
            g_final = norm_final if l == depth - 1 else None
            xb = _ffn(xb, norm_ffn2[l], w2g[l], w2u[l], w2d[l], g_final)
        outs.append(xb)
    return outs[0][None] if b == 1 else jnp.stack(outs, axis=0)
```

```python
import functools
import math

import jax
import jax.numpy as jnp
from jax import lax
from jax.experimental import pallas as pl
from jax.experimental.pallas import tpu as pltpu

F32 = jnp.float32
BF16 = jnp.bfloat16

EPS = 1e-5
NEG_INF = -1e30
ROPE_THETA = 500000.0
DIFF_HEAD_DIM = 64
HEAD_W = 2 * DIFF_HEAD_DIM
ROT_DIM = DIFF_HEAD_DIM // 4
CONV_K = 3
LOG2E = math.log2(math.e)

LANES = 128
SUBLANES = 8
VMEM_LIMIT = 56 * 1024 * 1024


def _params(*sem):
    return pltpu.CompilerParams(dimension_semantics=sem, vmem_limit_bytes=VMEM_LIMIT)


def _rms(x, g):
    ms = jnp.mean(x * x, axis=-1, keepdims=True)
    return x * lax.rsqrt(ms + EPS) * g


def _ffn_kernel(*refs, final_norm):
    if final_norm:
        x_ref, g_ref, wg_ref, wu_ref, wd_ref, gf_ref, o_ref, h_ref = refs
    else:
        x_ref, g_ref, wg_ref, wu_ref, wd_ref, o_ref, h_ref = refs
    f = pl.program_id(1)

    @pl.when(f == 0)
    def _():
        h_ref[...] = _rms(x_ref[...], g_ref[...]).astype(BF16)
        o_ref[...] = jnp.zeros_like(o_ref)

    h = h_ref[...]
    gate = jnp.dot(h, wg_ref[...], preferred_element_type=F32)
    up = jnp.dot(h, wu_ref[...], preferred_element_type=F32)
    act = (gate * (1.0 / (1.0 + jnp.exp(-gate))) * up).astype(BF16)
    o_ref[...] += jnp.dot(act, wd_ref[...], preferred_element_type=F32)

    @pl.when(f == pl.num_programs(1) - 1)
    def _():
        y = x_ref[...] + 0.5 * o_ref[...]
        if final_norm:
            y = _rms(y, gf_ref[...])
        o_ref[...] = y


def _ffn(x, g, wg, wu, wd, g_final=None, *, tm=512, tf=512):
    s, d = x.shape
    dff = wg.shape[1]
    tm, tf = min(tm, s), min(tf, dff)
    assert s % tm == 0 and dff % tf == 0
    final_norm = g_final is not None
    in_specs = [
        pl.BlockSpec((tm, d), lambda i, f: (i, 0)),
        pl.BlockSpec((1, d), lambda i, f: (0, 0)),
        pl.BlockSpec((d, tf), lambda i, f: (0, f)),
        pl.BlockSpec((d, tf), lambda i, f: (0, f)),
        pl.BlockSpec((tf, d), lambda i, f: (f, 0)),
    ]
    args = [x, g.reshape(1, d), wg, wu, wd]
    if final_norm:
        in_specs.append(pl.BlockSpec((1, d), lambda i, f: (0, 0)))
        args.append(g_final.reshape(1, d))
    return pl.pallas_call(
        functools.partial(_ffn_kernel, final_norm=final_norm),
        grid=(s // tm, dff // tf),
        in_specs=in_specs,
        out_specs=pl.BlockSpec((tm, d), lambda i, f: (i, 0)),
        out_shape=jax.ShapeDtypeStruct((s, d), F32),
        scratch_shapes=[pltpu.VMEM((tm, d), BF16)],
        compiler_params=_params("parallel", "arbitrary"),
        name="ffn_final" if final_norm else "ffn",
    )(*args)


def _qkv_kernel(x_ref, g_ref, w_ref, c_ref, s1_ref, s2_ref, qt_ref, k_ref, vt_ref, hn_ref,
                *, q_scale):
    j = pl.program_id(1)

    @pl.when(j == 0)
    def _():
        hn_ref[...] = _rms(x_ref[...], g_ref[...]).astype(BF16)

    acc = jnp.dot(hn_ref[...], w_ref[...], preferred_element_type=F32)
    n_chunks = acc.shape[1] // LANES

    def rope(t):
        up = pltpu.roll(t, LANES - ROT_DIM // 2, axis=1)
        dn = pltpu.roll(t, ROT_DIM // 2, axis=1)
        return t * c_ref[...] + up * s1_ref[...] + dn * s2_ref[...]

    @pl.when(j == 0)
    def _():
        for c in range(n_chunks):
            cs = slice(c * LANES, (c + 1) * LANES)
            qt_ref[cs, :] = (rope(acc[:, cs]) * q_scale).T.astype(BF16)

    @pl.when(j == 1)
    def _():
        for c in range(n_chunks):
            cs = slice(c * LANES, (c + 1) * LANES)
            k_ref[:, cs] = rope(acc[:, cs]).astype(BF16)

    @pl.when(j == 2)
    def _():
        for c in range(n_chunks):
            cs = slice(c * LANES, (c + 1) * LANES)
            vt_ref[cs, :] = acc[:, cs].T.astype(BF16)


def _qkv(x, g, w_in, rope_c, rope_s1, rope_s2, attn_w, *, tm=512):
    s, d = x.shape
    tm = min(tm, s)
    assert s % tm == 0 and attn_w % LANES == 0
    q_scale = DIFF_HEAD_DIM ** -0.5 * LOG2E
    tab = pl.BlockSpec((tm, LANES), lambda i, j: (i, 0))
    return pl.pallas_call(
        functools.partial(_qkv_kernel, q_scale=q_scale),
        grid=(s // tm, 3),
        in_specs=[
            pl.BlockSpec((tm, d), lambda i, j: (i, 0)),
            pl.BlockSpec((1, d), lambda i, j: (0, 0)),
            pl.BlockSpec((d, attn_w), lambda i, j: (0, j)),
            tab, tab, tab,
        ],
        out_specs=[
            pl.BlockSpec((attn_w, tm), lambda i, j: (0, i)),
            pl.BlockSpec((tm, attn_w), lambda i, j: (i, 0)),
            pl.BlockSpec((None, attn_w, tm), lambda i, j: (i, 0, 0)),
            pl.BlockSpec((tm, d), lambda i, j: (i, 0)),
        ],
        out_shape=[
            jax.ShapeDtypeStruct((attn_w, s), BF16),
            jax.ShapeDtypeStruct((s, attn_w), BF16),
            jax.ShapeDtypeStruct((s // tm, attn_w, tm), BF16),
            jax.ShapeDtypeStruct((s, d), BF16),
        ],
        compiler_params=_params("parallel", "arbitrary"),
        name="qkv_rope",
    )(x, g.reshape(1, d), w_in, rope_c, rope_s1, rope_s2)


def _conv_kernel(hn_ref, wb_ref, wc_ref, wu_ref, cw_ref, o_ref, z_ref):
    i = pl.program_id(1)
    tm = hn_ref.shape[0]

    @pl.when(i == 0)
    def _():
        z_ref[0:SUBLANES, :] = jnp.zeros((SUBLANES, z_ref.shape[1]), F32)

    h = hn_ref[...]
    gate_b = jnp.dot(h, wb_ref[...], preferred_element_type=F32)
    gate_c = jnp.dot(h, wc_ref[...], preferred_element_type=F32)
    u = jnp.dot(h, wu_ref[...], preferred_element_type=F32)
    z = gate_c * u
    z_ref[SUBLANES:SUBLANES + tm, :] = z
    z1 = z_ref[SUBLANES - 1:SUBLANES - 1 + tm, :]
    z2 = z_ref[SUBLANES - 2:SUBLANES - 2 + tm, :]
    cw = cw_ref[...]
    y = cw[0:1, :] * z2 + cw[1:2, :] * z1 + cw[2:3, :] * z
    o_ref[...] = (gate_b * y).astype(BF16)
    z_ref[0:SUBLANES, :] = z_ref[tm:tm + SUBLANES, :]


def _conv_branch(hn, w_in, conv_w, attn_w, *, tm=512, tc=512):
    s, d = hn.shape
    cw_width = conv_w.shape[1]
    tm, tc = min(tm, s), min(tc, cw_width)
    assert s % tm == 0 and cw_width % tc == 0 and (3 * attn_w) % tc == 0
    nj = cw_width // tc
    off = 3 * attn_w // tc

    def wspec(k):
        return pl.BlockSpec((d, tc), lambda j, i: (0, off + k * nj + j))

    return pl.pallas_call(
        _conv_kernel,
        grid=(nj, s // tm),
        in_specs=[
            pl.BlockSpec((tm, d), lambda j, i: (i, 0)),
            wspec(0), wspec(1), wspec(2),
            pl.BlockSpec((CONV_K, tc), lambda j, i: (0, j)),
        ],
        out_specs=pl.BlockSpec((tm, tc), lambda j, i: (i, j)),
        out_shape=jax.ShapeDtypeStruct((s, cw_width), BF16),
        scratch_shapes=[pltpu.VMEM((tm + SUBLANES, tc), F32)],
        compiler_params=_params("parallel", "arbitrary"),
        name="gated_conv",
    )(hn, w_in, w_in, w_in, conv_w)


def _attn_kernel(li_ref, qt_ref, k_ref, vt_ref, lq1_ref, lk1_ref, lq2_ref, lk2_ref, w_ref,
                 o_ref, qs_ref, m_ref, l_ref, acc_ref, *, cw):
    i = pl.program_id(1)
    tq = qt_ref.shape[1]
    tk = vt_ref.shape[2]

    qt = qt_ref[...]
    row = lax.broadcasted_iota(jnp.int32, qt.shape, 0)
    zero = jnp.zeros_like(qt)
    qs_ref[:, 0:tq] = jnp.where(row < DIFF_HEAD_DIM, qt, zero)
    qs_ref[:, tq:2 * tq] = jnp.where(row >= DIFF_HEAD_DIM, qt, zero)
    m_ref[...] = jnp.full(m_ref.shape, NEG_INF, F32)
    l_ref[...] = jnp.zeros(l_ref.shape, F32)
    acc_ref[...] = jnp.zeros(acc_ref.shape, F32)

    def step(j, masked):
        k = k_ref[pl.ds(pl.multiple_of(j * tk, tk), tk), :]
        vt = vt_ref[j]
        for c in range(2 * tq // cw):
            cs = slice(c * cw, (c + 1) * cw)
            st = jnp.dot(k, qs_ref[:, cs], preferred_element_type=F32)
            if masked:
                key = lax.broadcasted_iota(jnp.int32, (tk, cw), 0)
                qry = lax.broadcasted_iota(jnp.int32, (tk, cw), 1) + (c * cw) % tq
                st = jnp.where(key <= qry, st, NEG_INF)
            m_prev = m_ref[:, cs]
            m_new = jnp.maximum(m_prev, jnp.max(st, axis=0, keepdims=True))
            alpha = jnp.exp2(m_prev - m_new)
            pt = jnp.exp2(st - m_new)
            l_ref[:, cs] = alpha * l_ref[:, cs] + jnp.sum(pt, axis=0, keepdims=True)
            acc_ref[:, cs] = alpha * acc_ref[:, cs] + jnp.dot(
                vt, pt.astype(BF16), preferred_element_type=F32)
            m_ref[:, cs] = m_new

    def body(j, carry):
        step(j, False)
        return carry

    lax.fori_loop(0, i, body, 0)
    step(i, True)

    lam = (jnp.exp(jnp.sum(lq1_ref[...] * lk1_ref[...], axis=-1, keepdims=True))
           - jnp.exp(jnp.sum(lq2_ref[...] * lk2_ref[...], axis=-1, keepdims=True))
           + li_ref[0])
    inv_l = 1.0 / l_ref[...]
    o1 = acc_ref[:, 0:tq] * inv_l[:, 0:tq]
    o2 = acc_ref[:, tq:2 * tq] * inv_l[:, tq:2 * tq]
    ot = o1 - lam * o2
    ms = jnp.mean(ot * ot, axis=0, keepdims=True)
    ot = ot * lax.rsqrt(ms + EPS) * w_ref[...] * (1.0 - li_ref[0])
    o_ref[...] = ot.T.astype(BF16)


def _attention(qt, k, vt, lam_init, lq1, lk1, lq2, lk2, subln_w, n_heads, *, cw=512):
    s = k.shape[0]
    tk = vt.shape[2]
    tq = tk
    cw = min(cw, tq)
    assert s % tq == 0 and tq % cw == 0
    dh = DIFF_HEAD_DIM
    lspec = pl.BlockSpec((1, dh), lambda h, i: (0, 0))
    return pl.pallas_call(
        functools.partial(_attn_kernel, cw=cw),
        grid=(n_heads, s // tq),
        in_specs=[
            pl.BlockSpec(memory_space=pltpu.SMEM),
            pl.BlockSpec((HEAD_W, tq), lambda h, i: (h, i)),
            pl.BlockSpec((s, HEAD_W), lambda h, i: (0, h)),
            pl.BlockSpec((s // tk, HEAD_W, tk), lambda h, i: (0, h, 0)),
            lspec, lspec, lspec, lspec,
            pl.BlockSpec((HEAD_W, 1), lambda h, i: (0, 0)),
        ],
        out_specs=pl.BlockSpec((tq, HEAD_W), lambda h, i: (i, h)),
        out_shape=jax.ShapeDtypeStruct((s, n_heads * HEAD_W), BF16),
        scratch_shapes=[
            pltpu.VMEM((HEAD_W, 2 * tq), BF16),
            pltpu.VMEM((1, 2 * tq), F32),
            pltpu.VMEM((1, 2 * tq), F32),
            pltpu.VMEM((HEAD_W, 2 * tq), F32),
        ],
        compiler_params=_params("parallel", "arbitrary"),
        name="diff_attn",
    )(lam_init, qt, k, vt, lq1.reshape(1, dh), lk1.reshape(1, dh),
      lq2.reshape(1, dh), lk2.reshape(1, dh), subln_w.reshape(HEAD_W, 1))


def _out_kernel(x_ref, a_ref, c_ref, wa_ref, wc_ref, o_ref):
    o_ref[...] = (x_ref[...]
                  + jnp.dot(a_ref[...], wa_ref[...], preferred_element_type=F32)
                  + jnp.dot(c_ref[...], wc_ref[...], preferred_element_type=F32))


def _out_proj(x, attn, conv, w_out, *, tm=512):
    s, d = x.shape
    aw, cw = attn.shape[1], conv.shape[1]
    tm = min(tm, s)
    assert s % tm == 0 and aw == cw
    return pl.pallas_call(
        _out_kernel,
        grid=(s // tm,),
        in_specs=[
            pl.BlockSpec((tm, d), lambda i: (i, 0)),
            pl.BlockSpec((tm, aw), lambda i: (i, 0)),
            pl.BlockSpec((tm, cw), lambda i: (i, 0)),
            pl.BlockSpec((aw, d), lambda i: (0, 0)),
            pl.BlockSpec((cw, d), lambda i: (1, 0)),
        ],
        out_specs=pl.BlockSpec((tm, d), lambda i: (i, 0)),
        out_shape=jax.ShapeDtypeStruct((s, d), F32),
        compiler_params=_params("parallel"),
        name="out_proj",
    )(x, attn, conv, w_out, w_out)


def _rope_lane_tables(seq):
    half = ROT_DIM // 2
    pos = jnp.arange(seq, dtype=F32)
    inv_freq = ROPE_THETA ** (-jnp.arange(0, ROT_DIM, 2, dtype=F32) / ROT_DIM)
    ang = pos[:, None] * inv_freq[None, :]
    cos, sin = jnp.cos(ang), jnp.sin(ang)
    rest = DIFF_HEAD_DIM - ROT_DIM
    ones = jnp.ones((seq, rest), F32)
    c = jnp.concatenate([cos, cos, ones], axis=-1)
    s_up = jnp.concatenate([-sin, jnp.zeros((seq, half + rest), F32)], axis=-1)
    s_dn = jnp.concatenate([jnp.zeros((seq, half), F32), sin, jnp.zeros((seq, rest), F32)], axis=-1)
    rep = LANES // DIFF_HEAD_DIM
    return jnp.tile(c, (1, rep)), jnp.tile(s_up, (1, rep)), jnp.tile(s_dn, (1, rep))


def kernel(x, norm_ffn1, ffn1_w_gate, ffn1_w_up, ffn1_w_down, norm_mix, w_in, conv_w,
           lambda_q1, lambda_k1, lambda_q2, lambda_k2, subln_w, w_out,
           norm_ffn2, ffn2_w_gate, ffn2_w_up, ffn2_w_down, norm_final):
    b, s, d = x.shape
    depth = norm_ffn1.shape[0]
    attn_w = d // 2
    n_heads = attn_w // HEAD_W
    bf = lambda w: w.astype(BF16)
    w1g, w1u, w1d = bf(ffn1_w_gate), bf(ffn1_w_up), bf(ffn1_w_down)
    w2g, w2u, w2d = bf(ffn2_w_gate), bf(ffn2_w_up), bf(ffn2_w_down)
    w_in_b, w_out_b = bf(w_in), bf(w_out)
    rope_c, rope_s1, rope_s2 = _rope_lane_tables(s)

    outs = []
    for bi in range(b):
        xb = x[bi]
        for l in range(depth):
            lam_init = jnp.full((1,), 0.8 - 0.6 * math.exp(-0.3 * l), F32)
            xb = _ffn(xb, norm_ffn1[l], w1g[l], w1u[l], w1d[l])
            qt, k, vt, hn = _qkv(xb, norm_mix[l], w_in_b[l], rope_c, rope_s1, rope_s2, attn_w)
            conv = _conv_branch(hn, w_in_b[l], conv_w[l], attn_w)
            attn = _attention(qt, k, vt, lam_init, lambda_q1[l], lambda_k1[l], lambda_q2[l],
                              lambda_k2[l], subln_w[l], n_heads)
            xb = _out_proj(xb, attn, conv, w_out_b[l])
            g_final = norm_final if l == depth - 1 else None
            xb = _ffn(xb, norm_ffn2[l], w2g[l], w2u[l], w2d[l], g_final)
        outs.append(xb)
    return outs[0][None] if b == 1 else jnp.stack(outs, axis=0)
```

```python
import functools
import math

import jax
import jax.numpy as jnp
from jax import lax
from jax.experimental import pallas as pl
from jax.experimental.pallas import tpu as pltpu

F32 = jnp.float32
BF16 = jnp.bfloat16

EPS = 1e-5
NEG_INF = -1e30
ROPE_THETA = 500000.0
DIFF_HEAD_DIM = 64
HEAD_W = 2 * DIFF_HEAD_DIM
ROT_DIM = DIFF_HEAD_DIM // 4
CONV_K = 3
LOG2E = math.log2(math.e)

LANES = 128
SUBLANES = 8
VMEM_LIMIT = 56 * 1024 * 1024


def _params(*sem):
    return pltpu.CompilerParams(dimension_semantics=sem, vmem_limit_bytes=VMEM_LIMIT)


def _rms(x, g):
    ms = jnp.mean(x * x, axis=-1, keepdims=True)
    return x * lax.rsqrt(ms + EPS) * g


def _ffn_kernel(*refs, final_norm):
    if final_norm:
        x_ref, g_ref, wg_ref, wu_ref, wd_ref, gf_ref, o_ref, h_ref = refs
    else:
        x_ref, g_ref, wg_ref, wu_ref, wd_ref, o_ref, h_ref = refs
    f = pl.program_id(1)

    @pl.when(f == 0)
    def _():
        h_ref[...] = _rms(x_ref[...], g_ref[...]).astype(BF16)
        o_ref[...] = jnp.zeros_like(o_ref)

    h = h_ref[...]
    gate = jnp.dot(h, wg_ref[...], preferred_element_type=F32)
    up = jnp.dot(h, wu_ref[...], preferred_element_type=F32)
    act = (gate * (1.0 / (1.0 + jnp.exp(-gate))) * up).astype(BF16)
    o_ref[...] += jnp.dot(act, wd_ref[...], preferred_element_type=F32)

    @pl.when(f == pl.num_programs(1) - 1)
    def _():
        y = x_ref[...] + 0.5 * o_ref[...]
        if final_norm:
            y = _rms(y, gf_ref[...])
        o_ref[...] = y


def _ffn(x, g, wg, wu, wd, layer, g_final=None, *, tm=512, tf=512):
    s, d = x.shape
    dff = wg.shape[2]
    tm, tf = min(tm, s), min(tf, dff)
    assert s % tm == 0 and dff % tf == 0
    final_norm = g_final is not None
    in_specs = [
        pl.BlockSpec((tm, d), lambda i, f: (i, 0)),
        pl.BlockSpec((1, d), lambda i, f: (0, 0)),
        pl.BlockSpec((None, d, tf), lambda i, f: (layer, 0, f)),
        pl.BlockSpec((None, d, tf), lambda i, f: (layer, 0, f)),
        pl.BlockSpec((None, tf, d), lambda i, f: (layer, f, 0)),
    ]
    args = [x, g.reshape(1, d), wg, wu, wd]
    if final_norm:
        in_specs.append(pl.BlockSpec((1, d), lambda i, f: (0, 0)))
        args.append(g_final.reshape(1, d))
    return pl.pallas_call(
        functools.partial(_ffn_kernel, final_norm=final_norm),
        grid=(s // tm, dff // tf),
        in_specs=in_specs,
        out_specs=pl.BlockSpec((tm, d), lambda i, f: (i, 0)),
        out_shape=jax.ShapeDtypeStruct((s, d), F32),
        scratch_shapes=[pltpu.VMEM((tm, d), BF16)],
        compiler_params=_params("parallel", "arbitrary"),
        name="ffn_final" if final_norm else "ffn",
    )(*args)


def _qkv_kernel(x_ref, g_ref, w_ref, c_ref, s1_ref, s2_ref, qt_ref, k_ref, vt_ref, hn_ref,
                *, q_scale):
    j = pl.program_id(1)

    @pl.when(j == 0)
    def _():
        hn_ref[...] = _rms(x_ref[...], g_ref[...]).astype(BF16)

    acc = jnp.dot(hn_ref[...], w_ref[...], preferred_element_type=F32)
    n_chunks = acc.shape[1] // LANES

    def rope(t):
        up = pltpu.roll(t, LANES - ROT_DIM // 2, axis=1)
        dn = pltpu.roll(t, ROT_DIM // 2, axis=1)
        return t * c_ref[...] + up * s1_ref[...] + dn * s2_ref[...]

    @pl.when(j == 0)
    def _():
        for c in range(n_chunks):
            cs = slice(c * LANES, (c + 1) * LANES)
            qt_ref[cs, :] = (rope(acc[:, cs]) * q_scale).T.astype(BF16)

    @pl.when(j == 1)
    def _():
        for c in range(n_chunks):
            cs = slice(c * LANES, (c + 1) * LANES)
            k_ref[:, cs] = rope(acc[:, cs]).astype(BF16)

    @pl.when(j == 2)
    def _():
        for c in range(n_chunks):
            cs = slice(c * LANES, (c + 1) * LANES)
            vt_ref[cs, :] = acc[:, cs].T.astype(BF16)


def _qkv(x, g, w_in, layer, rope_c, rope_s1, rope_s2, attn_w, *, tm=512):
    s, d = x.shape
    tm = min(tm, s)
    assert s % tm == 0 and attn_w % LANES == 0
    q_scale = DIFF_HEAD_DIM ** -0.5 * LOG2E
    tab = pl.BlockSpec((tm, LANES), lambda i, j: (i, 0))
    return pl.pallas_call(
        functools.partial(_qkv_kernel, q_scale=q_scale),
        grid=(s // tm, 3),
        in_specs=[
            pl.BlockSpec((tm, d), lambda i, j: (i, 0)),
            pl.BlockSpec((1, d), lambda i, j: (0, 0)),
            pl.BlockSpec((None, d, attn_w), lambda i, j: (layer, 0, j)),
            tab, tab, tab,
        ],
        out_specs=[
            pl.BlockSpec((attn_w, tm), lambda i, j: (0, i)),
            pl.BlockSpec((tm, attn_w), lambda i, j: (i, 0)),
            pl.BlockSpec((None, attn_w, tm), lambda i, j: (i, 0, 0)),
            pl.BlockSpec((tm, d), lambda i, j: (i, 0)),
        ],
        out_shape=[
            jax.ShapeDtypeStruct((attn_w, s), BF16),
            jax.ShapeDtypeStruct((s, attn_w), BF16),
            jax.ShapeDtypeStruct((s // tm, attn_w, tm), BF16),
            jax.ShapeDtypeStruct((s, d), BF16),
        ],
        compiler_params=_params("parallel", "arbitrary"),
        name="qkv_rope",
    )(x, g.reshape(1, d), w_in, rope_c, rope_s1, rope_s2)


def _conv_kernel(hn_ref, wb_ref, wc_ref, wu_ref, cw_ref, o_ref, z_ref):
    i = pl.program_id(1)
    tm = hn_ref.shape[0]

    @pl.when(i == 0)
    def _():
        z_ref[0:SUBLANES, :] = jnp.zeros((SUBLANES, z_ref.shape[1]), F32)

    h = hn_ref[...]
    gate_b = jnp.dot(h, wb_ref[...], preferred_element_type=F32)
    gate_c = jnp.dot(h, wc_ref[...], preferred_element_type=F32)
    u = jnp.dot(h, wu_ref[...], preferred_element_type=F32)
    z = gate_c * u
    z_ref[SUBLANES:SUBLANES + tm, :] = z
    z1 = z_ref[SUBLANES - 1:SUBLANES - 1 + tm, :]
    z2 = z_ref[SUBLANES - 2:SUBLANES - 2 + tm, :]
    cw = cw_ref[...]
    y = cw[0:1, :] * z2 + cw[1:2, :] * z1 + cw[2:3, :] * z
    o_ref[...] = (gate_b * y).astype(BF16)
    z_ref[0:SUBLANES, :] = z_ref[tm:tm + SUBLANES, :]


def _conv_branch(hn, w_in, layer, conv_w, attn_w, *, tm=512, tc=512):
    s, d = hn.shape
    cw_width = conv_w.shape[1]
    tm, tc = min(tm, s), min(tc, cw_width)
    assert s % tm == 0 and cw_width % tc == 0 and (3 * attn_w) % tc == 0
    nj = cw_width // tc
    off = 3 * attn_w // tc

    def wspec(k):
        return pl.BlockSpec((None, d, tc), lambda j, i: (layer, 0, off + k * nj + j))

    return pl.pallas_call(
        _conv_kernel,
        grid=(nj, s // tm),
        in_specs=[
            pl.BlockSpec((tm, d), lambda j, i: (i, 0)),
            wspec(0), wspec(1), wspec(2),
            pl.BlockSpec((CONV_K, tc), lambda j, i: (0, j)),
        ],
        out_specs=pl.BlockSpec((tm, tc), lambda j, i: (i, j)),
        out_shape=jax.ShapeDtypeStruct((s, cw_width), BF16),
        scratch_shapes=[pltpu.VMEM((tm + SUBLANES, tc), F32)],
        compiler_params=_params("parallel", "arbitrary"),
        name="gated_conv",
    )(hn, w_in, w_in, w_in, conv_w)


def _attn_kernel(li_ref, qt_ref, k_ref, vt_ref, lq1_ref, lk1_ref, lq2_ref, lk2_ref, w_ref,
                 o_ref, qs_ref, m_ref, l_ref, acc_ref, st_ref, *, cw, ks, depth):
    i = pl.program_id(1)
    tq = qt_ref.shape[1]
    tk = vt_ref.shape[2]

    qt = qt_ref[...]
    row = lax.broadcasted_iota(jnp.int32, qt.shape, 0)
    zero = jnp.zeros_like(qt)
    qs_ref[:, 0:tq] = jnp.where(row < DIFF_HEAD_DIM, qt, zero)
    qs_ref[:, tq:2 * tq] = jnp.where(row >= DIFF_HEAD_DIM, qt, zero)
    m_ref[...] = jnp.full(m_ref.shape, NEG_INF, F32)
    l_ref[...] = jnp.zeros(l_ref.shape, F32)
    acc_ref[...] = jnp.zeros(acc_ref.shape, F32)

    units = [(t, c) for t in range(tk // ks) for c in range(2 * tq // cw)]

    def scores(j, t, c):
        k = k_ref[pl.ds(pl.multiple_of(j * tk + t * ks, ks), ks), :]
        return jnp.dot(k, qs_ref[:, c * cw:(c + 1) * cw], preferred_element_type=F32)

    def stage_scores(j):
        for d in range(depth):
            st_ref[d] = scores(j, *units[d])

    def softmax_pv(j, t, c, st, masked):
        cs = slice(c * cw, (c + 1) * cw)
        if masked:
            key = lax.broadcasted_iota(jnp.int32, (ks, cw), 0) + t * ks
            qry = lax.broadcasted_iota(jnp.int32, (ks, cw), 1) + (c * cw) % tq
            st = jnp.where(key <= qry, st, NEG_INF)
        m_prev = m_ref[:, cs]
        m_new = jnp.maximum(m_prev, jnp.max(st, axis=0, keepdims=True))
        alpha = jnp.exp2(m_prev - m_new)
        pt = jnp.exp2(st - m_new)
        l_ref[:, cs] = alpha * l_ref[:, cs] + jnp.sum(pt, axis=0, keepdims=True)
        vt = vt_ref[j, :, t * ks:(t + 1) * ks]
        acc_ref[:, cs] = alpha * acc_ref[:, cs] + jnp.dot(
            vt, pt.astype(BF16), preferred_element_type=F32)
        m_ref[:, cs] = m_new

    def step(j, masked):
        pending = [st_ref[d] for d in range(depth)]
        for n, u in enumerate(units):
            st = pending.pop(0)
            ahead = n + depth
            if ahead < len(units):
                pending.append(scores(j, *units[ahead]))
            elif not masked:
                st_ref[ahead - len(units)] = scores(j + 1, *units[ahead - len(units)])
            softmax_pv(j, *u, st, masked)

    def body(j, carry):
        step(j, False)
        return carry

    stage_scores(0)
    lax.fori_loop(0, i, body, 0)
    step(i, True)

    lam = (jnp.exp(jnp.sum(lq1_ref[...] * lk1_ref[...], axis=-1, keepdims=True))
           - jnp.exp(jnp.sum(lq2_ref[...] * lk2_ref[...], axis=-1, keepdims=True))
           + li_ref[0])
    inv_l = 1.0 / l_ref[...]
    o1 = acc_ref[:, 0:tq] * inv_l[:, 0:tq]
    o2 = acc_ref[:, tq:2 * tq] * inv_l[:, tq:2 * tq]
    ot = o1 - lam * o2
    ms = jnp.mean(ot * ot, axis=0, keepdims=True)
    ot = ot * lax.rsqrt(ms + EPS) * w_ref[...] * (1.0 - li_ref[0])
    o_ref[...] = ot.T.astype(BF16)


def _attention(qt, k, vt, lam_init, lq1, lk1, lq2, lk2, subln_w, n_heads, *, cw=512, ks=256,
               depth=2):
    s = k.shape[0]
    tk = vt.shape[2]
    tq = tk
    cw, ks = min(cw, tq), min(ks, tk)
    assert s % tq == 0 and tq % cw == 0 and tk % ks == 0
    dh = DIFF_HEAD_DIM
    lspec = pl.BlockSpec((1, dh), lambda h, i: (0, 0))
    return pl.pallas_call(
        functools.partial(_attn_kernel, cw=cw, ks=ks, depth=depth),
        grid=(n_heads, s // tq),
        in_specs=[
            pl.BlockSpec(memory_space=pltpu.SMEM),
            pl.BlockSpec((HEAD_W, tq), lambda h, i: (h, i)),
            pl.BlockSpec((s, HEAD_W), lambda h, i: (0, h)),
            pl.BlockSpec((s // tk, HEAD_W, tk), lambda h, i: (0, h, 0)),
            lspec, lspec, lspec, lspec,
            pl.BlockSpec((HEAD_W, 1), lambda h, i: (0, 0)),
        ],
        out_specs=pl.BlockSpec((tq, HEAD_W), lambda h, i: (i, h)),
        out_shape=jax.ShapeDtypeStruct((s, n_heads * HEAD_W), BF16),
        scratch_shapes=[
            pltpu.VMEM((HEAD_W, 2 * tq), BF16),
            pltpu.VMEM((1, 2 * tq), F32),
            pltpu.VMEM((1, 2 * tq), F32),
            pltpu.VMEM((HEAD_W, 2 * tq), F32),
            pltpu.VMEM((depth, ks, cw), F32),
        ],
        compiler_params=_params("parallel", "arbitrary"),
        name="diff_attn",
    )(lam_init, qt, k, vt, lq1.reshape(1, dh), lk1.reshape(1, dh),
      lq2.reshape(1, dh), lk2.reshape(1, dh), subln_w.reshape(HEAD_W, 1))


def _out_kernel(x_ref, a_ref, c_ref, wa_ref, wc_ref, o_ref):
    o_ref[...] = (x_ref[...]
                  + jnp.dot(a_ref[...], wa_ref[...], preferred_element_type=F32)
                  + jnp.dot(c_ref[...], wc_ref[...], preferred_element_type=F32))


def _out_proj(x, attn, conv, w_out, layer, *, tm=512):
    s, d = x.shape
    aw, cw = attn.shape[1], conv.shape[1]
    tm = min(tm, s)
    assert s % tm == 0 and aw == cw
    return pl.pallas_call(
        _out_kernel,
        grid=(s // tm,),
        in_specs=[
            pl.BlockSpec((tm, d), lambda i: (i, 0)),
            pl.BlockSpec((tm, aw), lambda i: (i, 0)),
            pl.BlockSpec((tm, cw), lambda i: (i, 0)),
            pl.BlockSpec((None, aw, d), lambda i: (layer, 0, 0)),
            pl.BlockSpec((None, cw, d), lambda i: (layer, 1, 0)),
        ],
        out_specs=pl.BlockSpec((tm, d), lambda i: (i, 0)),
        out_shape=jax.ShapeDtypeStruct((s, d), F32),
        compiler_params=_params("parallel"),
        name="out_proj",
    )(x, attn, conv, w_out, w_out)


def _rope_lane_tables(seq):
    half = ROT_DIM // 2
    pos = jnp.arange(seq, dtype=F32)
    inv_freq = ROPE_THETA ** (-jnp.arange(0, ROT_DIM, 2, dtype=F32) / ROT_DIM)
    ang = pos[:, None] * inv_freq[None, :]
    cos, sin = jnp.cos(ang), jnp.sin(ang)
    rest = DIFF_HEAD_DIM - ROT_DIM
    ones = jnp.ones((seq, rest), F32)
    c = jnp.concatenate([cos, cos, ones], axis=-1)
    s_up = jnp.concatenate([-sin, jnp.zeros((seq, half + rest), F32)], axis=-1)
    s_dn = jnp.concatenate([jnp.zeros((seq, half), F32), sin, jnp.zeros((seq, rest), F32)], axis=-1)
    rep = LANES // DIFF_HEAD_DIM
    return jnp.tile(c, (1, rep)), jnp.tile(s_up, (1, rep)), jnp.tile(s_dn, (1, rep))


def kernel(x, norm_ffn1, ffn1_w_gate, ffn1_w_up, ffn1_w_down, norm_mix, w_in, conv_w,
           lambda_q1, lambda_k1, lambda_q2, lambda_k2, subln_w, w_out,
           norm_ffn2, ffn2_w_gate, ffn2_w_up, ffn2_w_down, norm_final):
    b, s, d = x.shape
    depth = norm_ffn1.shape[0]
    attn_w = d // 2
    n_heads = attn_w // HEAD_W
    bf = lambda w: w.astype(BF16)
    w1g, w1u, w1d = bf(ffn1_w_gate), bf(ffn1_w_up), bf(ffn1_w_down)
    w2g, w2u, w2d = bf(ffn2_w_gate), bf(ffn2_w_up), bf(ffn2_w_down)
    w_in_b, w_out_b = bf(w_in), bf(w_out)
    rope_c, rope_s1, rope_s2 = _rope_lane_tables(s)

    outs = []
    for bi in range(b):
        xb = x[bi]
        for l in range(depth):
            lam_init = jnp.full((1,), 0.8 - 0.6 * math.exp(-0.3 * l), F32)
            xb = _ffn(xb, norm_ffn1[l], w1g, w1u, w1d, l)
            qt, k, vt, hn = _qkv(xb, norm_mix[l], w_in_b, l, rope_c, rope_s1, rope_s2, attn_w)
            conv = _conv_branch(hn, w_in_b, l, conv_w[l], attn_w)
            attn = _attention(qt, k, vt, lam_init, lambda_q1[l], lambda_k1[l], lambda_q2[l],
                              lambda_k2[l], subln_w[l], n_heads)
            xb = _out_proj(xb, attn, conv, w_out_b, l)
            g_final = norm_final if l == depth - 1 else None
            xb = _ffn(xb, norm_ffn2[l], w2g, w2u, w2d, l, g_final)
        outs.append(xb)
    return outs[0][None] if b == 1 else jnp.stack(outs, axis=0)
```

```python
import functools
import math

import jax
import jax.numpy as jnp
from jax import lax
from jax.experimental import pallas as pl
from jax.experimental.pallas import tpu as pltpu

F32 = jnp.float32
BF16 = jnp.bfloat16

EPS = 1e-5
NEG_INF = -1e30
ROPE_THETA = 500000.0
DIFF_HEAD_DIM = 64
HEAD_W = 2 * DIFF_HEAD_DIM
ROT_DIM = DIFF_HEAD_DIM // 4
CONV_K = 3
LOG2E = math.log2(math.e)

VT_PAD = 16
VT_ROWS = HEAD_W + VT_PAD
LANES = 128
SUBLANES = 8
VMEM_LIMIT = 56 * 1024 * 1024


def _params(*sem):
    return pltpu.CompilerParams(dimension_semantics=sem, vmem_limit_bytes=VMEM_LIMIT)


def _rms(x, g):
    ms = jnp.mean(x * x, axis=-1, keepdims=True)
    return x * lax.rsqrt(ms + EPS) * g


def _ffn_kernel(*refs, final_norm):
    if final_norm:
        x_ref, g_ref, wg_ref, wu_ref, wd_ref, gf_ref, o_ref, h_ref = refs
    else:
        x_ref, g_ref, wg_ref, wu_ref, wd_ref, o_ref, h_ref = refs
    f = pl.program_id(1)

    @pl.when(f == 0)
    def _():
        h_ref[...] = _rms(x_ref[...], g_ref[...]).astype(BF16)
        o_ref[...] = jnp.zeros_like(o_ref)

    h = h_ref[...]
    gate = jnp.dot(h, wg_ref[...], preferred_element_type=F32)
    up = jnp.dot(h, wu_ref[...], preferred_element_type=F32)
    act = (gate * (1.0 / (1.0 + jnp.exp(-gate))) * up).astype(BF16)
    o_ref[...] += jnp.dot(act, wd_ref[...], preferred_element_type=F32)

    @pl.when(f == pl.num_programs(1) - 1)
    def _():
        y = x_ref[...] + 0.5 * o_ref[...]
        if final_norm:
            y = _rms(y, gf_ref[...])
        o_ref[...] = y


def _ffn(x, g, wg, wu, wd, layer, g_final=None, *, tm=512, tf=512):
    s, d = x.shape
    dff = wg.shape[2]
    tm, tf = min(tm, s), min(tf, dff)
    assert s % tm == 0 and dff % tf == 0
    final_norm = g_final is not None
    in_specs = [
        pl.BlockSpec((tm, d), lambda i, f: (i, 0)),
        pl.BlockSpec((1, d), lambda i, f: (0, 0)),
        pl.BlockSpec((None, d, tf), lambda i, f: (layer, 0, f)),
        pl.BlockSpec((None, d, tf), lambda i, f: (layer, 0, f)),
        pl.BlockSpec((None, tf, d), lambda i, f: (layer, f, 0)),
    ]
    args = [x, g.reshape(1, d), wg, wu, wd]
    if final_norm:
        in_specs.append(pl.BlockSpec((1, d), lambda i, f: (0, 0)))
        args.append(g_final.reshape(1, d))
    return pl.pallas_call(
        functools.partial(_ffn_kernel, final_norm=final_norm),
        grid=(s // tm, dff // tf),
        in_specs=in_specs,
        out_specs=pl.BlockSpec((tm, d), lambda i, f: (i, 0)),
        out_shape=jax.ShapeDtypeStruct((s, d), F32),
        scratch_shapes=[pltpu.VMEM((tm, d), BF16)],
        compiler_params=_params("parallel", "arbitrary"),
        name="ffn_final" if final_norm else "ffn",
    )(*args)


def _qkv_kernel(x_ref, g_ref, w_ref, c_ref, s1_ref, s2_ref, qt_ref, k_ref, vt_ref, hn_ref,
                *, q_scale, attn_w, tn):
    hn_ref[...] = _rms(x_ref[...], g_ref[...]).astype(BF16)
    hn = hn_ref[...]
    tm = hn.shape[0]
    heads_per_chunk = tn // HEAD_W

    def rope(t):
        up = pltpu.roll(t, LANES - ROT_DIM // 2, axis=1)
        dn = pltpu.roll(t, ROT_DIM // 2, axis=1)
        return t * c_ref[...] + up * s1_ref[...] + dn * s2_ref[...]

    ones = jnp.ones((VT_PAD, tm), BF16)
    for n in range(3 * attn_w // tn):
        acc = jnp.dot(hn, w_ref[:, n * tn:(n + 1) * tn], preferred_element_type=F32)
        kind, base = divmod(n * tn, attn_w)
        for c in range(heads_per_chunk):
            t = acc[:, c * HEAD_W:(c + 1) * HEAD_W]
            head = base // HEAD_W + c
            if kind == 0:
                qt_ref[head * HEAD_W:(head + 1) * HEAD_W, :] = (rope(t) * q_scale).T.astype(BF16)
            elif kind == 1:
                k_ref[:, head * HEAD_W:(head + 1) * HEAD_W] = rope(t).astype(BF16)
            else:
                r0 = head * VT_ROWS
                vt_ref[r0:r0 + HEAD_W, :] = t.T.astype(BF16)
                vt_ref[r0 + HEAD_W:r0 + VT_ROWS, :] = ones


def _qkv(x, g, w_in, layer, rope_c, rope_s1, rope_s2, attn_w, *, tm=512, tn=512):
    s, d = x.shape
    tm = min(tm, s)
    tn = min(tn, attn_w)
    assert s % tm == 0 and attn_w % tn == 0 and tn % HEAD_W == 0 and HEAD_W == LANES
    q_scale = DIFF_HEAD_DIM ** -0.5 * LOG2E
    n_heads = attn_w // HEAD_W
    tab = pl.BlockSpec((tm, LANES), lambda i: (i, 0))
    return pl.pallas_call(
        functools.partial(_qkv_kernel, q_scale=q_scale, attn_w=attn_w, tn=tn),
        grid=(s // tm,),
        in_specs=[
            pl.BlockSpec((tm, d), lambda i: (i, 0)),
            pl.BlockSpec((1, d), lambda i: (0, 0)),
            pl.BlockSpec((None, d, 3 * attn_w), lambda i: (layer, 0, 0),
                         pipeline_mode=pl.Buffered(1)),
            tab, tab, tab,
        ],
        out_specs=[
            pl.BlockSpec((attn_w, tm), lambda i: (0, i)),
            pl.BlockSpec((tm, attn_w), lambda i: (i, 0)),
            pl.BlockSpec((None, n_heads * VT_ROWS, tm), lambda i: (i, 0, 0)),
            pl.BlockSpec((tm, d), lambda i: (i, 0)),
        ],
        out_shape=[
            jax.ShapeDtypeStruct((attn_w, s), BF16),
            jax.ShapeDtypeStruct((s, attn_w), BF16),
            jax.ShapeDtypeStruct((s // tm, n_heads * VT_ROWS, tm), BF16),
            jax.ShapeDtypeStruct((s, d), BF16),
        ],
        compiler_params=_params("arbitrary"),
        name="qkv_rope",
    )(x, g.reshape(1, d), w_in, rope_c, rope_s1, rope_s2)


def _conv_kernel(hn_ref, wb_ref, wc_ref, wu_ref, cw_ref, o_ref, z_ref):
    i = pl.program_id(1)
    tm = hn_ref.shape[0]

    @pl.when(i == 0)
    def _():
        z_ref[0:SUBLANES, :] = jnp.zeros((SUBLANES, z_ref.shape[1]), F32)

    h = hn_ref[...]
    gate_b = jnp.dot(h, wb_ref[...], preferred_element_type=F32)
    gate_c = jnp.dot(h, wc_ref[...], preferred_element_type=F32)
    u = jnp.dot(h, wu_ref[...], preferred_element_type=F32)
    z = gate_c * u
    z_ref[SUBLANES:SUBLANES + tm, :] = z
    z1 = z_ref[SUBLANES - 1:SUBLANES - 1 + tm, :]
    z2 = z_ref[SUBLANES - 2:SUBLANES - 2 + tm, :]
    cw = cw_ref[...]
    y = cw[0:1, :] * z2 + cw[1:2, :] * z1 + cw[2:3, :] * z
    o_ref[...] = (gate_b * y).astype(BF16)
    z_ref[0:SUBLANES, :] = z_ref[tm:tm + SUBLANES, :]


def _conv_branch(hn, w_in, layer, conv_w, attn_w, *, tm=512, tc=512):
    s, d = hn.shape
    cw_width = conv_w.shape[1]
    tm, tc = min(tm, s), min(tc, cw_width)
    assert s % tm == 0 and cw_width % tc == 0 and (3 * attn_w) % tc == 0
    nj = cw_width // tc
    off = 3 * attn_w // tc

    def wspec(k):
        return pl.BlockSpec((None, d, tc), lambda j, i: (layer, 0, off + k * nj + j))

    return pl.pallas_call(
        _conv_kernel,
        grid=(nj, s // tm),
        in_specs=[
            pl.BlockSpec((tm, d), lambda j, i: (i, 0)),
            wspec(0), wspec(1), wspec(2),
            pl.BlockSpec((CONV_K, tc), lambda j, i: (0, j)),
        ],
        out_specs=pl.BlockSpec((tm, tc), lambda j, i: (i, j)),
        out_shape=jax.ShapeDtypeStruct((s, cw_width), BF16),
        scratch_shapes=[pltpu.VMEM((tm + SUBLANES, tc), F32)],
        compiler_params=_params("parallel", "arbitrary"),
        name="gated_conv",
    )(hn, w_in, w_in, w_in, conv_w)


def _attn_kernel(li_ref, qt_ref, k_ref, vt_ref, lq1_ref, lk1_ref, lq2_ref, lk2_ref, w_ref,
                 o_ref, qs_ref, m_ref, acc_ref, st_ref, *, ks, depth):
    i = pl.program_id(1)
    tq = qt_ref.shape[1]
    tk = vt_ref.shape[2]

    qt = qt_ref[...]
    row = lax.broadcasted_iota(jnp.int32, qt.shape, 0)
    zero = jnp.zeros_like(qt)
    qs_ref[:, 0:tq] = jnp.where(row < DIFF_HEAD_DIM, qt, zero)
    qs_ref[:, tq:2 * tq] = jnp.where(row >= DIFF_HEAD_DIM, qt, zero)
    m_ref[...] = jnp.full(m_ref.shape, NEG_INF, F32)
    acc_ref[...] = jnp.zeros(acc_ref.shape, F32)

    nq = tq // tk
    all_units = [(t, c) for t in range(tk // ks) for c in range(2 * nq)]

    def diag_units(d):
        return [(t, c) for (t, c) in all_units if c % nq >= d]

    def scores(j, t, c):
        k = k_ref[pl.ds(pl.multiple_of(j * tk + t * ks, ks), ks), :]
        return jnp.dot(k, qs_ref[:, c * tk:(c + 1) * tk], preferred_element_type=F32)

    def softmax_pv(j, t, c, st, triangular):
        cs = slice(c * tk, (c + 1) * tk)
        if triangular:
            key = lax.broadcasted_iota(jnp.int32, (ks, tk), 0) + t * ks
            qry = lax.broadcasted_iota(jnp.int32, (ks, tk), 1)
            st = jnp.where(key <= qry, st, NEG_INF)
        m_prev = m_ref[:, cs]
        m_new = jnp.maximum(m_prev, jnp.max(st, axis=0, keepdims=True))
        alpha = jnp.exp2(m_prev - m_new)
        pt = jnp.exp2(st - m_new)
        vt = vt_ref[j, :, t * ks:(t + 1) * ks]
        acc_ref[:, cs] = alpha * acc_ref[:, cs] + jnp.dot(
            vt, pt.astype(BF16), preferred_element_type=F32)
        m_ref[:, cs] = m_new

    def step(j, units, diag, next_units):
        pending = [st_ref[d] for d in range(depth)]
        for n, (t, c) in enumerate(units):
            st = pending.pop(0)
            ahead = n + depth
            if ahead < len(units):
                pending.append(scores(j, *units[ahead]))
            elif next_units is not None:
                st_ref[ahead - len(units)] = scores(j + 1, *next_units[ahead - len(units)])
            softmax_pv(j, t, c, st, diag is not None and c % nq == diag)

    def body(j, carry):
        step(j, all_units, None, all_units)
        return carry

    for d in range(depth):
        st_ref[d] = scores(0, *all_units[d])
    lax.fori_loop(0, i * nq, body, 0)
    for d in range(nq):
        step(i * nq + d, diag_units(d), d, diag_units(d + 1) if d + 1 < nq else None)

    lam = (jnp.exp(jnp.sum(lq1_ref[...] * lk1_ref[...], axis=-1, keepdims=True))
           - jnp.exp(jnp.sum(lq2_ref[...] * lk2_ref[...], axis=-1, keepdims=True))
           + li_ref[0])
    inv_l = 1.0 / acc_ref[HEAD_W:HEAD_W + 1, :]
    o1 = acc_ref[0:HEAD_W, 0:tq] * inv_l[:, 0:tq]
    o2 = acc_ref[0:HEAD_W, tq:2 * tq] * inv_l[:, tq:2 * tq]
    ot = o1 - lam * o2
    ms = jnp.mean(ot * ot, axis=0, keepdims=True)
    ot = ot * lax.rsqrt(ms + EPS) * w_ref[...] * (1.0 - li_ref[0])
    o_ref[...] = ot.T.astype(BF16)


def _attention(qt, k, vt, lam_init, lq1, lk1, lq2, lk2, subln_w, n_heads, *, tq=2048, ks=256,
               depth=2):
    s = k.shape[0]
    tk = vt.shape[2]
    tq, ks = max(min(tq, s), tk), min(ks, tk)
    assert s % tq == 0 and tq % tk == 0 and tk % ks == 0
    dh = DIFF_HEAD_DIM
    lspec = pl.BlockSpec((1, dh), lambda h, i: (0, 0))
    return pl.pallas_call(
        functools.partial(_attn_kernel, ks=ks, depth=depth),
        grid=(n_heads, s // tq),
        in_specs=[
            pl.BlockSpec(memory_space=pltpu.SMEM),
            pl.BlockSpec((HEAD_W, tq), lambda h, i: (h, i)),
            pl.BlockSpec((s, HEAD_W), lambda h, i: (0, h)),
            pl.BlockSpec((s // tk, VT_ROWS, tk), lambda h, i: (0, h, 0)),
            lspec, lspec, lspec, lspec,
            pl.BlockSpec((HEAD_W, 1), lambda h, i: (0, 0)),
        ],
        out_specs=pl.BlockSpec((tq, HEAD_W), lambda h, i: (i, h)),
        out_shape=jax.ShapeDtypeStruct((s, n_heads * HEAD_W), BF16),
        scratch_shapes=[
            pltpu.VMEM((HEAD_W, 2 * tq), BF16),
            pltpu.VMEM((1, 2 * tq), F32),
            pltpu.VMEM((VT_ROWS, 2 * tq), F32),
            pltpu.VMEM((depth, ks, tk), F32),
        ],
        compiler_params=_params("arbitrary", "arbitrary"),
        name="diff_attn",
    )(lam_init, qt, k, vt, lq1.reshape(1, dh), lk1.reshape(1, dh),
      lq2.reshape(1, dh), lk2.reshape(1, dh), subln_w.reshape(HEAD_W, 1))


def _out_kernel(x_ref, a_ref, c_ref, wa_ref, wc_ref, o_ref):
    o_ref[...] = (x_ref[...]
                  + jnp.dot(a_ref[...], wa_ref[...], preferred_element_type=F32)
                  + jnp.dot(c_ref[...], wc_ref[...], preferred_element_type=F32))


def _out_proj(x, attn, conv, w_out, layer, *, tm=512):
    s, d = x.shape
    aw, cw = attn.shape[1], conv.shape[1]
    tm = min(tm, s)
    assert s % tm == 0 and aw == cw
    return pl.pallas_call(
        _out_kernel,
        grid=(s // tm,),
        in_specs=[
            pl.BlockSpec((tm, d), lambda i: (i, 0)),
            pl.BlockSpec((tm, aw), lambda i: (i, 0)),
            pl.BlockSpec((tm, cw), lambda i: (i, 0)),
            pl.BlockSpec((None, aw, d), lambda i: (layer, 0, 0)),
            pl.BlockSpec((None, cw, d), lambda i: (layer, 1, 0)),
        ],
        out_specs=pl.BlockSpec((tm, d), lambda i: (i, 0)),
        out_shape=jax.ShapeDtypeStruct((s, d), F32),
        compiler_params=_params("parallel"),
        name="out_proj",
    )(x, attn, conv, w_out, w_out)


def _rope_lane_tables(seq):
    half = ROT_DIM // 2
    pos = jnp.arange(seq, dtype=F32)
    inv_freq = ROPE_THETA ** (-jnp.arange(0, ROT_DIM, 2, dtype=F32) / ROT_DIM)
    ang = pos[:, None] * inv_freq[None, :]
    cos, sin = jnp.cos(ang), jnp.sin(ang)
    rest = DIFF_HEAD_DIM - ROT_DIM
    ones = jnp.ones((seq, rest), F32)
    c = jnp.concatenate([cos, cos, ones], axis=-1)
    s_up = jnp.concatenate([-sin, jnp.zeros((seq, half + rest), F32)], axis=-1)
    s_dn = jnp.concatenate([jnp.zeros((seq, half), F32), sin, jnp.zeros((seq, rest), F32)], axis=-1)
    rep = LANES // DIFF_HEAD_DIM
    return jnp.tile(c, (1, rep)), jnp.tile(s_up, (1, rep)), jnp.tile(s_dn, (1, rep))


def kernel(x, norm_ffn1, ffn1_w_gate, ffn1_w_up, ffn1_w_down, norm_mix, w_in, conv_w,
           lambda_q1, lambda_k1, lambda_q2, lambda_k2, subln_w, w_out,
           norm_ffn2, ffn2_w_gate, ffn2_w_up, ffn2_w_down, norm_final):
    b, s, d = x.shape
    depth = norm_ffn1.shape[0]
    attn_w = d // 2
    n_heads = attn_w // HEAD_W
    bf = lambda w: w.astype(BF16)
    w1g, w1u, w1d = bf(ffn1_w_gate), bf(ffn1_w_up), bf(ffn1_w_down)
    w2g, w2u, w2d = bf(ffn2_w_gate), bf(ffn2_w_up), bf(ffn2_w_down)
    w_in_b, w_out_b = bf(w_in), bf(w_out)
    rope_c, rope_s1, rope_s2 = _rope_lane_tables(s)

    outs = []
    for bi in range(b):
        xb = x[bi]
        for l in range(depth):
            lam_init = jnp.full((1,), 0.8 - 0.6 * math.exp(-0.3 * l), F32)
            xb = _ffn(xb, norm_ffn1[l], w1g, w1u, w1d, l)
            qt, k, vt, hn = _qkv(xb, norm_mix[l], w_in_b, l, rope_c, rope_s1, rope_s2, attn_w)
            conv = _conv_branch(hn, w_in_b, l, conv_w[l], attn_w)
            attn = _attention(qt, k, vt, lam_init, lambda_q1[l], lambda_k1[l], lambda_q2[l],
                              lambda_k2[l], subln_w[l], n_heads)
            xb = _out_proj(xb, attn, conv, w_out_b, l)
            g_final = norm_final if l == depth - 1 else None
            xb = _ffn(xb, norm_ffn2[l], w2g, w2u, w2d, l, g_final)
        outs.append(xb)
    return outs[0][None] if b == 1 else jnp.stack(outs, axis=0)
```

```python
import functools
import math

import jax
import jax.numpy as jnp
from jax import lax
from jax.experimental import pallas as pl
from jax.experimental.pallas import tpu as pltpu

F32 = jnp.float32
BF16 = jnp.bfloat16

EPS = 1e-5
NEG_INF = -1e30
ROPE_THETA = 500000.0
DIFF_HEAD_DIM = 64
HEAD_W = 2 * DIFF_HEAD_DIM
ROT_DIM = DIFF_HEAD_DIM // 4
CONV_K = 3
LOG2E = math.log2(math.e)

VT_PAD = 16
VT_ROWS = HEAD_W + VT_PAD
LANES = 128
SUBLANES = 8
VMEM_LIMIT = 56 * 1024 * 1024


def _params(*sem):
    return pltpu.CompilerParams(dimension_semantics=sem, vmem_limit_bytes=VMEM_LIMIT)


def _rms(x, g):
    ms = jnp.mean(x * x, axis=-1, keepdims=True)
    return x * lax.rsqrt(ms + EPS) * g


def _ffn_kernel(*refs, final_norm, rows):
    if final_norm:
        x_ref, g_ref, wg_ref, wu_ref, wd_ref, gf_ref, o_ref, h_ref = refs
    else:
        x_ref, g_ref, wg_ref, wu_ref, wd_ref, o_ref, h_ref = refs
    f = pl.program_id(1)
    tm = x_ref.shape[0]

    @pl.when(f == 0)
    def _():
        def norm_rows(r, carry):
            rs = pl.ds(pl.multiple_of(r * rows, rows), rows)
            h_ref[rs, :] = _rms(x_ref[rs, :], g_ref[...]).astype(BF16)
            return carry

        lax.fori_loop(0, tm // rows, norm_rows, 0)
        o_ref[...] = jnp.zeros_like(o_ref)

    h = h_ref[...]
    gate = jnp.dot(h, wg_ref[...], preferred_element_type=F32)
    up = jnp.dot(h, wu_ref[...], preferred_element_type=F32)
    act = (gate * (1.0 / (1.0 + jnp.exp(-gate))) * up).astype(BF16)
    o_ref[...] += jnp.dot(act, wd_ref[...], preferred_element_type=F32)

    @pl.when(f == pl.num_programs(1) - 1)
    def _():
        def finish_rows(r, carry):
            rs = pl.ds(pl.multiple_of(r * rows, rows), rows)
            y = x_ref[rs, :] + 0.5 * o_ref[rs, :]
            if final_norm:
                y = _rms(y, gf_ref[...])
            o_ref[rs, :] = y
            return carry

        lax.fori_loop(0, tm // rows, finish_rows, 0)


def _ffn(x, g, wg, wu, wd, layer, g_final=None, *, tm=1024, tf=512):
    s, d = x.shape
    dff = wg.shape[2]
    tm, tf = min(tm, s), min(tf, dff)
    assert s % tm == 0 and dff % tf == 0
    final_norm = g_final is not None
    in_specs = [
        pl.BlockSpec((tm, d), lambda i, f: (i, 0)),
        pl.BlockSpec((1, d), lambda i, f: (0, 0)),
        pl.BlockSpec((None, d, tf), lambda i, f: (layer, 0, f)),
        pl.BlockSpec((None, d, tf), lambda i, f: (layer, 0, f)),
        pl.BlockSpec((None, tf, d), lambda i, f: (layer, f, 0)),
    ]
    args = [x, g.reshape(1, d), wg, wu, wd]
    if final_norm:
        in_specs.append(pl.BlockSpec((1, d), lambda i, f: (0, 0)))
        args.append(g_final.reshape(1, d))
    return pl.pallas_call(
        functools.partial(_ffn_kernel, final_norm=final_norm, rows=min(256, tm)),
        grid=(s // tm, dff // tf),
        in_specs=in_specs,
        out_specs=pl.BlockSpec((tm, d), lambda i, f: (i, 0)),
        out_shape=jax.ShapeDtypeStruct((s, d), F32),
        scratch_shapes=[pltpu.VMEM((tm, d), BF16)],
        compiler_params=_params("parallel", "arbitrary"),
        name="ffn_final" if final_norm else "ffn",
    )(*args)


def _qkv_kernel(x_ref, g_ref, w_ref, c_ref, s1_ref, s2_ref, qt_ref, k_ref, vt_ref, hn_ref,
                *, q_scale, attn_w, tn):
    hn_ref[...] = _rms(x_ref[...], g_ref[...]).astype(BF16)
    hn = hn_ref[...]
    tm = hn.shape[0]
    heads_per_chunk = tn // HEAD_W

    def rope(t):
        up = pltpu.roll(t, LANES - ROT_DIM // 2, axis=1)
        dn = pltpu.roll(t, ROT_DIM // 2, axis=1)
        return t * c_ref[...] + up * s1_ref[...] + dn * s2_ref[...]

    ones = jnp.ones((VT_PAD, tm), BF16)
    for n in range(3 * attn_w // tn):
        acc = jnp.dot(hn, w_ref[:, n * tn:(n + 1) * tn], preferred_element_type=F32)
        kind, base = divmod(n * tn, attn_w)
        for c in range(heads_per_chunk):
            t = acc[:, c * HEAD_W:(c + 1) * HEAD_W]
            head = base // HEAD_W + c
            if kind == 0:
                qt_ref[head * HEAD_W:(head + 1) * HEAD_W, :] = (rope(t) * q_scale).T.astype(BF16)
            elif kind == 1:
                k_ref[:, head * HEAD_W:(head + 1) * HEAD_W] = rope(t).astype(BF16)
            else:
                r0 = head * VT_ROWS
                vt_ref[r0:r0 + HEAD_W, :] = t.T.astype(BF16)
                vt_ref[r0 + HEAD_W:r0 + VT_ROWS, :] = ones


def _qkv(x, g, w_in, layer, rope_c, rope_s1, rope_s2, attn_w, *, tm, tn=512):
    s, d = x.shape
    tm = min(tm, s)
    tn = min(tn, attn_w)
    assert s % tm == 0 and attn_w % tn == 0 and tn % HEAD_W == 0 and HEAD_W == LANES
    q_scale = DIFF_HEAD_DIM ** -0.5 * LOG2E
    n_heads = attn_w // HEAD_W
    tab = pl.BlockSpec((tm, LANES), lambda i: (i, 0))
    return pl.pallas_call(
        functools.partial(_qkv_kernel, q_scale=q_scale, attn_w=attn_w, tn=tn),
        grid=(s // tm,),
        in_specs=[
            pl.BlockSpec((tm, d), lambda i: (i, 0)),
            pl.BlockSpec((1, d), lambda i: (0, 0)),
            pl.BlockSpec((None, d, 3 * attn_w), lambda i: (layer, 0, 0),
                         pipeline_mode=pl.Buffered(1)),
            tab, tab, tab,
        ],
        out_specs=[
            pl.BlockSpec((attn_w, tm), lambda i: (0, i)),
            pl.BlockSpec((tm, attn_w), lambda i: (i, 0)),
            pl.BlockSpec((None, n_heads * VT_ROWS, tm), lambda i: (i, 0, 0)),
            pl.BlockSpec((tm, d), lambda i: (i, 0)),
        ],
        out_shape=[
            jax.ShapeDtypeStruct((attn_w, s), BF16),
            jax.ShapeDtypeStruct((s, attn_w), BF16),
            jax.ShapeDtypeStruct((s // tm, n_heads * VT_ROWS, tm), BF16),
            jax.ShapeDtypeStruct((s, d), BF16),
        ],
        compiler_params=_params("arbitrary"),
        name="qkv_rope",
    )(x, g.reshape(1, d), w_in, rope_c, rope_s1, rope_s2)


def _conv_kernel(hn_ref, wb_ref, wc_ref, wu_ref, cw_ref, o_ref, z_ref):
    i = pl.program_id(1)
    tm = hn_ref.shape[0]

    @pl.when(i == 0)
    def _():
        z_ref[0:SUBLANES, :] = jnp.zeros((SUBLANES, z_ref.shape[1]), F32)

    h = hn_ref[...]
    gate_b = jnp.dot(h, wb_ref[...], preferred_element_type=F32)
    gate_c = jnp.dot(h, wc_ref[...], preferred_element_type=F32)
    u = jnp.dot(h, wu_ref[...], preferred_element_type=F32)
    z = gate_c * u
    z_ref[SUBLANES:SUBLANES + tm, :] = z
    z1 = z_ref[SUBLANES - 1:SUBLANES - 1 + tm, :]
    z2 = z_ref[SUBLANES - 2:SUBLANES - 2 + tm, :]
    cw = cw_ref[...]
    y = cw[0:1, :] * z2 + cw[1:2, :] * z1 + cw[2:3, :] * z
    o_ref[...] = (gate_b * y).astype(BF16)
    z_ref[0:SUBLANES, :] = z_ref[tm:tm + SUBLANES, :]


def _conv_branch(hn, w_in, layer, conv_w, attn_w, *, tm=512, tc=512):
    s, d = hn.shape
    cw_width = conv_w.shape[1]
    tm, tc = min(tm, s), min(tc, cw_width)
    assert s % tm == 0 and cw_width % tc == 0 and (3 * attn_w) % tc == 0
    nj = cw_width // tc
    off = 3 * attn_w // tc

    def wspec(k):
        return pl.BlockSpec((None, d, tc), lambda j, i: (layer, 0, off + k * nj + j))

    return pl.pallas_call(
        _conv_kernel,
        grid=(nj, s // tm),
        in_specs=[
            pl.BlockSpec((tm, d), lambda j, i: (i, 0)),
            wspec(0), wspec(1), wspec(2),
            pl.BlockSpec((CONV_K, tc), lambda j, i: (0, j)),
        ],
        out_specs=pl.BlockSpec((tm, tc), lambda j, i: (i, j)),
        out_shape=jax.ShapeDtypeStruct((s, cw_width), BF16),
        scratch_shapes=[pltpu.VMEM((tm + SUBLANES, tc), F32)],
        compiler_params=_params("parallel", "arbitrary"),
        name="gated_conv",
    )(hn, w_in, w_in, w_in, conv_w)


def _attn_kernel(*refs, ks, depth, n_cast):
    (li_ref, qt_ref, k_ref, vt_ref, lq1_ref, lk1_ref, lq2_ref, lk2_ref, w_ref) = refs[:9]
    cast_in = refs[9:9 + n_cast]
    o_ref = refs[9 + n_cast]
    cast_out = refs[10 + n_cast:10 + 2 * n_cast]
    qs_ref, m_ref, acc_ref, st_ref = refs[10 + 2 * n_cast:]
    i = pl.program_id(1)

    for src, dst in zip(cast_in, cast_out):
        dst[...] = src[...].astype(BF16)

    tq = qt_ref.shape[1]
    tk = vt_ref.shape[2]

    qt = qt_ref[...]
    row = lax.broadcasted_iota(jnp.int32, qt.shape, 0)
    zero = jnp.zeros_like(qt)
    qs_ref[:, 0:tq] = jnp.where(row < DIFF_HEAD_DIM, qt, zero)
    qs_ref[:, tq:2 * tq] = jnp.where(row >= DIFF_HEAD_DIM, qt, zero)
    m_ref[...] = jnp.full(m_ref.shape, NEG_INF, F32)
    acc_ref[...] = jnp.zeros(acc_ref.shape, F32)

    nq = tq // tk
    all_units = [(t, c) for t in range(tk // ks) for c in range(2 * nq)]

    def diag_units(d):
        return [(t, c) for (t, c) in all_units if c % nq >= d]

    def scores(j, t, c):
        k = k_ref[pl.ds(pl.multiple_of(j * tk + t * ks, ks), ks), :]
        return jnp.dot(k, qs_ref[:, c * tk:(c + 1) * tk], preferred_element_type=F32)

    def softmax_pv(j, t, c, st, triangular):
        cs = slice(c * tk, (c + 1) * tk)
        if triangular:
            key = lax.broadcasted_iota(jnp.int32, (ks, tk), 0) + t * ks
            qry = lax.broadcasted_iota(jnp.int32, (ks, tk), 1)
            st = jnp.where(key <= qry, st, NEG_INF)
        m_prev = m_ref[:, cs]
        m_new = jnp.maximum(m_prev, jnp.max(st, axis=0, keepdims=True))
        alpha = jnp.exp2(m_prev - m_new)
        pt = jnp.exp2(st - m_new)
        vt = vt_ref[j, :, t * ks:(t + 1) * ks]
        acc_ref[:, cs] = alpha * acc_ref[:, cs] + jnp.dot(
            vt, pt.astype(BF16), preferred_element_type=F32)
        m_ref[:, cs] = m_new

    def step(j, units, diag, next_units):
        pending = [st_ref[d] for d in range(depth)]
        for n, (t, c) in enumerate(units):
            st = pending.pop(0)
            ahead = n + depth
            if ahead < len(units):
                pending.append(scores(j, *units[ahead]))
            elif next_units is not None:
                st_ref[ahead - len(units)] = scores(j + 1, *next_units[ahead - len(units)])
            softmax_pv(j, t, c, st, diag is not None and c % nq == diag)

    def body(j, carry):
        step(j, all_units, None, all_units)
        return carry

    for d in range(depth):
        st_ref[d] = scores(0, *all_units[d])
    lax.fori_loop(0, i * nq, body, 0)
    for d in range(nq):
        step(i * nq + d, diag_units(d), d, diag_units(d + 1) if d + 1 < nq else None)

    lam = (jnp.exp(jnp.sum(lq1_ref[...] * lk1_ref[...], axis=-1, keepdims=True))
           - jnp.exp(jnp.sum(lq2_ref[...] * lk2_ref[...], axis=-1, keepdims=True))
           + li_ref[0])
    inv_l = 1.0 / acc_ref[HEAD_W:HEAD_W + 1, :]
    o1 = acc_ref[0:HEAD_W, 0:tq] * inv_l[:, 0:tq]
    o2 = acc_ref[0:HEAD_W, tq:2 * tq] * inv_l[:, tq:2 * tq]
    ot = o1 - lam * o2
    ms = jnp.mean(ot * ot, axis=0, keepdims=True)
    ot = ot * lax.rsqrt(ms + EPS) * w_ref[...] * (1.0 - li_ref[0])
    o_ref[...] = ot.T.astype(BF16)


def _attn_tq(s, tk, tq=2048):
    return max(min(tq, s), tk)


def _castable(w, n_steps):
    rows = w.shape[1]
    return rows % n_steps == 0 and (rows // n_steps) % 16 == 0


def _attention(qt, k, vt, lam_init, lq1, lk1, lq2, lk2, subln_w, n_heads, cast=(), *, ks=256,
               depth=2):
    s = k.shape[0]
    tk = vt.shape[2]
    tq, ks = _attn_tq(s, tk), min(ks, tk)
    assert s % tq == 0 and tq % tk == 0 and tk % ks == 0
    dh = DIFF_HEAD_DIM
    ni = s // tq
    n_steps = n_heads * ni
    lspec = pl.BlockSpec((1, dh), lambda h, i: (0, 0))
    cast_in_specs, cast_out_specs, cast_out_shapes = [], [], []
    for w, layer in cast:
        rows, cols = w.shape[1] // n_steps, w.shape[2]
        cast_in_specs.append(
            pl.BlockSpec((None, rows, cols), lambda h, i, layer=layer: (layer, h * ni + i, 0)))
        cast_out_specs.append(pl.BlockSpec((rows, cols), lambda h, i: (h * ni + i, 0)))
        cast_out_shapes.append(jax.ShapeDtypeStruct(w.shape[1:], BF16))
    outs = pl.pallas_call(
        functools.partial(_attn_kernel, ks=ks, depth=depth, n_cast=len(cast)),
        grid=(n_heads, ni),
        in_specs=[
            pl.BlockSpec(memory_space=pltpu.SMEM),
            pl.BlockSpec((HEAD_W, tq), lambda h, i: (h, i)),
            pl.BlockSpec((s, HEAD_W), lambda h, i: (0, h)),
            pl.BlockSpec((s // tk, VT_ROWS, tk), lambda h, i: (0, h, 0)),
            lspec, lspec, lspec, lspec,
            pl.BlockSpec((HEAD_W, 1), lambda h, i: (0, 0)),
        ] + cast_in_specs,
        out_specs=[pl.BlockSpec((tq, HEAD_W), lambda h, i: (i, h))] + cast_out_specs,
        out_shape=[jax.ShapeDtypeStruct((s, n_heads * HEAD_W), BF16)] + cast_out_shapes,
        scratch_shapes=[
            pltpu.VMEM((HEAD_W, 2 * tq), BF16),
            pltpu.VMEM((1, 2 * tq), F32),
            pltpu.VMEM((VT_ROWS, 2 * tq), F32),
            pltpu.VMEM((depth, ks, tk), F32),
        ],
        compiler_params=_params("arbitrary", "arbitrary"),
        name="diff_attn",
    )(lam_init, qt, k, vt, lq1.reshape(1, dh), lk1.reshape(1, dh),
      lq2.reshape(1, dh), lk2.reshape(1, dh), subln_w.reshape(HEAD_W, 1),
      *[w for w, _ in cast])
    return outs[0], outs[1:]


def _out_kernel(x_ref, a_ref, c_ref, wa_ref, wc_ref, o_ref):
    o_ref[...] = (x_ref[...]
                  + jnp.dot(a_ref[...], wa_ref[...], preferred_element_type=F32)
                  + jnp.dot(c_ref[...], wc_ref[...], preferred_element_type=F32))


def _out_proj(x, attn, conv, w_out, layer, *, tm=512):
    s, d = x.shape
    aw, cw = attn.shape[1], conv.shape[1]
    tm = min(tm, s)
    assert s % tm == 0 and aw == cw
    return pl.pallas_call(
        _out_kernel,
        grid=(s // tm,),
        in_specs=[
            pl.BlockSpec((tm, d), lambda i: (i, 0)),
            pl.BlockSpec((tm, aw), lambda i: (i, 0)),
            pl.BlockSpec((tm, cw), lambda i: (i, 0)),
            pl.BlockSpec((None, aw, d), lambda i: (layer, 0, 0)),
            pl.BlockSpec((None, cw, d), lambda i: (layer, 1, 0)),
        ],
        out_specs=pl.BlockSpec((tm, d), lambda i: (i, 0)),
        out_shape=jax.ShapeDtypeStruct((s, d), F32),
        compiler_params=_params("parallel"),
        name="out_proj",
    )(x, attn, conv, w_out, w_out)


def _rope_lane_tables(seq):
    half = ROT_DIM // 2
    pos = jnp.arange(seq, dtype=F32)
    inv_freq = ROPE_THETA ** (-jnp.arange(0, ROT_DIM, 2, dtype=F32) / ROT_DIM)
    ang = pos[:, None] * inv_freq[None, :]
    cos, sin = jnp.cos(ang), jnp.sin(ang)
    rest = DIFF_HEAD_DIM - ROT_DIM
    ones = jnp.ones((seq, rest), F32)
    c = jnp.concatenate([cos, cos, ones], axis=-1)
    s_up = jnp.concatenate([-sin, jnp.zeros((seq, half + rest), F32)], axis=-1)
    s_dn = jnp.concatenate([jnp.zeros((seq, half), F32), sin, jnp.zeros((seq, rest), F32)], axis=-1)
    rep = LANES // DIFF_HEAD_DIM
    return jnp.tile(c, (1, rep)), jnp.tile(s_up, (1, rep)), jnp.tile(s_dn, (1, rep))


def kernel(x, norm_ffn1, ffn1_w_gate, ffn1_w_up, ffn1_w_down, norm_mix, w_in, conv_w,
           lambda_q1, lambda_k1, lambda_q2, lambda_k2, subln_w, w_out,
           norm_ffn2, ffn2_w_gate, ffn2_w_up, ffn2_w_down, norm_final):
    b, s, d = x.shape
    depth = norm_ffn1.shape[0]
    attn_w = d // 2
    n_heads = attn_w // HEAD_W
    rope_c, rope_s1, rope_s2 = _rope_lane_tables(s)
    tm_qkv = min(512, s)
    n_steps = n_heads * (s // _attn_tq(s, tm_qkv))
    ffn1 = (ffn1_w_gate, ffn1_w_up, ffn1_w_down)
    ffn2 = (ffn2_w_gate, ffn2_w_up, ffn2_w_down)

    def bf_layer(w, l):
        return w[l].astype(BF16)[None]

    outs = []
    for bi in range(b):
        xb = x[bi]
        cur = {"ffn1": [bf_layer(w, 0) for w in ffn1], "w_in": bf_layer(w_in, 0),
               "w_out": bf_layer(w_out, 0)}
        for l in range(depth):
            lam_init = jnp.full((1,), 0.8 - 0.6 * math.exp(-0.3 * l), F32)
            xb = _ffn(xb, norm_ffn1[l], *cur["ffn1"], 0)
            qt, k, vt, hn = _qkv(xb, norm_mix[l], cur["w_in"], 0, rope_c, rope_s1, rope_s2,
                                 attn_w, tm=tm_qkv)
            conv = _conv_branch(hn, cur["w_in"], 0, conv_w[l], attn_w)
            wanted = [(w, l) for w in ffn2]
            if l + 1 < depth:
                wanted += [(w, l + 1) for w in ffn1] + [(w_in, l + 1), (w_out, l + 1)]
            in_kernel = [(w, ll) for w, ll in wanted if _castable(w, n_steps)]
            attn, casted = _attention(qt, k, vt, lam_init, lambda_q1[l], lambda_k1[l],
                                      lambda_q2[l], lambda_k2[l], subln_w[l], n_heads, in_kernel)
            casted = list(casted)
            ready = [casted.pop(0)[None] if _castable(w, n_steps) else bf_layer(w, ll)
                     for w, ll in wanted]
            xb = _out_proj(xb, attn, conv, cur["w_out"], 0)
            g_final = norm_final if l == depth - 1 else None
            xb = _ffn(xb, norm_ffn2[l], *ready[:3], 0, g_final)
            if l + 1 < depth:
                cur = {"ffn1": ready[3:6], "w_in": ready[6], "w_out": ready[7]}
        outs.append(xb)
    return outs[0][None] if b == 1 else jnp.stack(outs, axis=0)
```

```python
import functools
import math

import jax
import jax.numpy as jnp
from jax import lax
from jax.experimental import pallas as pl
from jax.experimental.pallas import tpu as pltpu

F32 = jnp.float32
BF16 = jnp.bfloat16

EPS = 1e-5
NEG_INF = -1e30
ROPE_THETA = 500000.0
DIFF_HEAD_DIM = 64
HEAD_W = 2 * DIFF_HEAD_DIM
ROT_DIM = DIFF_HEAD_DIM // 4
CONV_K = 3
LOG2E = math.log2(math.e)

VT_PAD = 16
VT_ROWS = HEAD_W + VT_PAD
LANES = 128
SUBLANES = 8
VMEM_LIMIT = 56 * 1024 * 1024


def _params(*sem):
    return pltpu.CompilerParams(dimension_semantics=sem, vmem_limit_bytes=VMEM_LIMIT)


def _rms(x, g):
    ms = jnp.mean(x * x, axis=-1, keepdims=True)
    return x * lax.rsqrt(ms + EPS) * g


def _ffn_kernel(*refs, final_norm, rows):
    if final_norm:
        x_ref, g_ref, wg_ref, wu_ref, wd_ref, gf_ref, o_ref, h_ref = refs
    else:
        x_ref, g_ref, wg_ref, wu_ref, wd_ref, o_ref, h_ref = refs
    f = pl.program_id(1)
    tm = x_ref.shape[0]

    @pl.when(f == 0)
    def _():
        def norm_rows(r, carry):
            rs = pl.ds(pl.multiple_of(r * rows, rows), rows)
            h_ref[rs, :] = _rms(x_ref[rs, :], g_ref[...]).astype(BF16)
            return carry

        lax.fori_loop(0, tm // rows, norm_rows, 0)
        o_ref[...] = jnp.zeros_like(o_ref)

    h = h_ref[...]
    gate = jnp.dot(h, wg_ref[...], preferred_element_type=F32)
    up = jnp.dot(h, wu_ref[...], preferred_element_type=F32)
    act = (gate * (1.0 / (1.0 + jnp.exp(-gate))) * up).astype(BF16)
    o_ref[...] += jnp.dot(act, wd_ref[...], preferred_element_type=F32)

    @pl.when(f == pl.num_programs(1) - 1)
    def _():
        def finish_rows(r, carry):
            rs = pl.ds(pl.multiple_of(r * rows, rows), rows)
            y = x_ref[rs, :] + 0.5 * o_ref[rs, :]
            if final_norm:
                y = _rms(y, gf_ref[...])
            o_ref[rs, :] = y
            return carry

        lax.fori_loop(0, tm // rows, finish_rows, 0)


def _ffn(x, g, wg, wu, wd, layer, g_final=None, *, tm=1024, tf=512):
    s, d = x.shape
    dff = wg.shape[2]
    tm, tf = min(tm, s), min(tf, dff)
    assert s % tm == 0 and dff % tf == 0
    final_norm = g_final is not None
    in_specs = [
        pl.BlockSpec((tm, d), lambda i, f: (i, 0)),
        pl.BlockSpec((1, d), lambda i, f: (0, 0)),
        pl.BlockSpec((None, d, tf), lambda i, f: (layer, 0, f)),
        pl.BlockSpec((None, d, tf), lambda i, f: (layer, 0, f)),
        pl.BlockSpec((None, tf, d), lambda i, f: (layer, f, 0)),
    ]
    args = [x, g.reshape(1, d), wg, wu, wd]
    if final_norm:
        in_specs.append(pl.BlockSpec((1, d), lambda i, f: (0, 0)))
        args.append(g_final.reshape(1, d))
    return pl.pallas_call(
        functools.partial(_ffn_kernel, final_norm=final_norm, rows=min(256, tm)),
        grid=(s // tm, dff // tf),
        in_specs=in_specs,
        out_specs=pl.BlockSpec((tm, d), lambda i, f: (i, 0)),
        out_shape=jax.ShapeDtypeStruct((s, d), F32),
        scratch_shapes=[pltpu.VMEM((tm, d), BF16)],
        compiler_params=_params("parallel", "arbitrary"),
        name="ffn_final" if final_norm else "ffn",
    )(*args)


def _inproj_kernel(x_ref, g_ref, w_ref, c_ref, s1_ref, s2_ref, cw_ref,
                   qt_ref, k_ref, vt_ref, conv_ref, h_ref, z_ref, *, q_scale, attn_w, tn):
    i = pl.program_id(0)
    tm = x_ref.shape[0]
    conv_w = conv_ref.shape[1]

    @pl.when(i == 0)
    def _():
        z_ref[0:SUBLANES, :] = jnp.zeros((SUBLANES, conv_w), F32)

    h_ref[...] = _rms(x_ref[...], g_ref[...]).astype(BF16)
    hn = h_ref[...]

    def proj(col):
        return jnp.dot(hn, w_ref[:, col:col + tn], preferred_element_type=F32)

    def rope(t):
        up = pltpu.roll(t, LANES - ROT_DIM // 2, axis=1)
        dn = pltpu.roll(t, ROT_DIM // 2, axis=1)
        return t * c_ref[...] + up * s1_ref[...] + dn * s2_ref[...]

    ones = jnp.ones((VT_PAD, tm), BF16)
    for n in range(3 * attn_w // tn):
        acc = proj(n * tn)
        kind, base = divmod(n * tn, attn_w)
        for c in range(tn // HEAD_W):
            t = acc[:, c * HEAD_W:(c + 1) * HEAD_W]
            head = base // HEAD_W + c
            if kind == 0:
                qt_ref[head * HEAD_W:(head + 1) * HEAD_W, :] = (rope(t) * q_scale).T.astype(BF16)
            elif kind == 1:
                k_ref[:, head * HEAD_W:(head + 1) * HEAD_W] = rope(t).astype(BF16)
            else:
                r0 = head * VT_ROWS
                vt_ref[r0:r0 + HEAD_W, :] = t.T.astype(BF16)
                vt_ref[r0 + HEAD_W:r0 + VT_ROWS, :] = ones

    for n in range(conv_w // tn):
        cs = slice(n * tn, (n + 1) * tn)
        gate_b = proj(3 * attn_w + n * tn)
        z = proj(3 * attn_w + conv_w + n * tn) * proj(3 * attn_w + 2 * conv_w + n * tn)
        z_ref[SUBLANES:SUBLANES + tm, cs] = z
        z1 = z_ref[SUBLANES - 1:SUBLANES - 1 + tm, cs]
        z2 = z_ref[SUBLANES - 2:SUBLANES - 2 + tm, cs]
        cw = cw_ref[:, cs]
        y = cw[0:1, :] * z2 + cw[1:2, :] * z1 + cw[2:3, :] * z
        conv_ref[:, cs] = (gate_b * y).astype(BF16)
        z_ref[0:SUBLANES, cs] = z_ref[tm:tm + SUBLANES, cs]


def _inproj(x, g, w_in, layer, rope_c, rope_s1, rope_s2, conv_w, attn_w, *, tm, tn=512):
    s, d = x.shape
    conv_width = conv_w.shape[1]
    tm = min(tm, s)
    tn = min(tn, attn_w, conv_width)
    assert s % tm == 0 and attn_w % tn == 0 and conv_width % tn == 0
    assert tn % HEAD_W == 0 and HEAD_W == LANES
    assert w_in.shape[2] == 3 * attn_w + 3 * conv_width
    q_scale = DIFF_HEAD_DIM ** -0.5 * LOG2E
    n_heads = attn_w // HEAD_W
    tab = pl.BlockSpec((tm, LANES), lambda i: (i, 0))
    return pl.pallas_call(
        functools.partial(_inproj_kernel, q_scale=q_scale, attn_w=attn_w, tn=tn),
        grid=(s // tm,),
        in_specs=[
            pl.BlockSpec((tm, d), lambda i: (i, 0)),
            pl.BlockSpec((1, d), lambda i: (0, 0)),
            pl.BlockSpec((None, d, w_in.shape[2]), lambda i: (layer, 0, 0),
                         pipeline_mode=pl.Buffered(1)),
            tab, tab, tab,
            pl.BlockSpec((CONV_K, conv_width), lambda i: (0, 0)),
        ],
        out_specs=[
            pl.BlockSpec((attn_w, tm), lambda i: (0, i)),
            pl.BlockSpec((tm, attn_w), lambda i: (i, 0)),
            pl.BlockSpec((None, n_heads * VT_ROWS, tm), lambda i: (i, 0, 0)),
            pl.BlockSpec((tm, conv_width), lambda i: (i, 0)),
        ],
        out_shape=[
            jax.ShapeDtypeStruct((attn_w, s), BF16),
            jax.ShapeDtypeStruct((s, attn_w), BF16),
            jax.ShapeDtypeStruct((s // tm, n_heads * VT_ROWS, tm), BF16),
            jax.ShapeDtypeStruct((s, conv_width), BF16),
        ],
        scratch_shapes=[
            pltpu.VMEM((tm, d), BF16),
            pltpu.VMEM((tm + SUBLANES, conv_width), F32),
        ],
        compiler_params=_params("arbitrary"),
        name="in_proj",
    )(x, g.reshape(1, d), w_in, rope_c, rope_s1, rope_s2, conv_w)


def _attn_kernel(*refs, ks, depth, n_cast):
    (li_ref, qt_ref, k_ref, vt_ref, lq1_ref, lk1_ref, lq2_ref, lk2_ref, w_ref) = refs[:9]
    cast_in = refs[9:9 + n_cast]
    o_ref = refs[9 + n_cast]
    cast_out = refs[10 + n_cast:10 + 2 * n_cast]
    qs_ref, m_ref, acc_ref, st_ref = refs[10 + 2 * n_cast:]
    i = pl.program_id(1)

    for src, dst in zip(cast_in, cast_out):
        dst[...] = src[...].astype(BF16)

    tq = qt_ref.shape[1]
    tk = vt_ref.shape[2]

    qt = qt_ref[...]
    row = lax.broadcasted_iota(jnp.int32, qt.shape, 0)
    zero = jnp.zeros_like(qt)
    qs_ref[:, 0:tq] = jnp.where(row < DIFF_HEAD_DIM, qt, zero)
    qs_ref[:, tq:2 * tq] = jnp.where(row >= DIFF_HEAD_DIM, qt, zero)
    m_ref[...] = jnp.full(m_ref.shape, NEG_INF, F32)
    acc_ref[...] = jnp.zeros(acc_ref.shape, F32)

    nq = tq // tk
    all_units = [(t, c) for t in range(tk // ks) for c in range(2 * nq)]

    def diag_units(d):
        return [(t, c) for (t, c) in all_units if c % nq >= d]

    def scores(j, t, c):
        k = k_ref[pl.ds(pl.multiple_of(j * tk + t * ks, ks), ks), :]
        return jnp.dot(k, qs_ref[:, c * tk:(c + 1) * tk], preferred_element_type=F32)

    def softmax_pv(j, t, c, st, triangular):
        cs = slice(c * tk, (c + 1) * tk)
        if triangular:
            key = lax.broadcasted_iota(jnp.int32, (ks, tk), 0) + t * ks
            qry = lax.broadcasted_iota(jnp.int32, (ks, tk), 1)
            st = jnp.where(key <= qry, st, NEG_INF)
        m_prev = m_ref[:, cs]
        m_new = jnp.maximum(m_prev, jnp.max(st, axis=0, keepdims=True))
        alpha = jnp.exp2(m_prev - m_new)
        pt = jnp.exp2(st - m_new)
        vt = vt_ref[j, :, t * ks:(t + 1) * ks]
        acc_ref[:, cs] = alpha * acc_ref[:, cs] + jnp.dot(
            vt, pt.astype(BF16), preferred_element_type=F32)
        m_ref[:, cs] = m_new

    def step(j, units, diag, next_units):
        pending = [st_ref[d] for d in range(depth)]
        for n, (dj, t, c) in enumerate(units):
            st = pending.pop(0)
            ahead = n + depth
            if ahead < len(units):
                aj, at, ac = units[ahead]
                pending.append(scores(j + aj, at, ac))
            elif next_units is not None:
                aj, at, ac = next_units[ahead - len(units)]
                st_ref[ahead - len(units)] = scores(j + aj, at, ac)
            softmax_pv(j + dj, t, c, st, diag is not None and c % nq == diag)

    def at_slab(dj, units):
        return [(dj, t, c) for (t, c) in units]

    per = 2 if nq % 2 == 0 else 1
    loop_units = [u for dj in range(per) for u in at_slab(dj, all_units)]

    def body(jj, carry):
        step(jj * per, loop_units, None, at_slab(per, all_units))
        return carry

    for d in range(depth):
        st_ref[d] = scores(0, *all_units[d])
    lax.fori_loop(0, i * nq // per, body, 0)
    for d in range(nq):
        nxt = at_slab(1, diag_units(d + 1)) if d + 1 < nq else None
        step(i * nq + d, at_slab(0, diag_units(d)), d, nxt)

    lam = (jnp.exp(jnp.sum(lq1_ref[...] * lk1_ref[...], axis=-1, keepdims=True))
           - jnp.exp(jnp.sum(lq2_ref[...] * lk2_ref[...], axis=-1, keepdims=True))
           + li_ref[0])
    inv_l = 1.0 / acc_ref[HEAD_W:HEAD_W + 1, :]
    o1 = acc_ref[0:HEAD_W, 0:tq] * inv_l[:, 0:tq]
    o2 = acc_ref[0:HEAD_W, tq:2 * tq] * inv_l[:, tq:2 * tq]
    ot = o1 - lam * o2
    ms = jnp.mean(ot * ot, axis=0, keepdims=True)
    ot = ot * lax.rsqrt(ms + EPS) * w_ref[...] * (1.0 - li_ref[0])
    o_ref[...] = ot.T.astype(BF16)


def _attn_tq(s, tk, tq=2048):
    return max(min(tq, s), tk)


def _castable(w, n_steps):
    rows = w.shape[1]
    return rows % n_steps == 0 and (rows // n_steps) % 16 == 0


def _attention(qt, k, vt, lam_init, lq1, lk1, lq2, lk2, subln_w, n_heads, cast=(), *, ks=256,
               depth=2):
    s = k.shape[0]
    tk = vt.shape[2]
    tq, ks = _attn_tq(s, tk), min(ks, tk)
    assert s % tq == 0 and tq % tk == 0 and tk % ks == 0
    dh = DIFF_HEAD_DIM
    ni = s // tq
    n_steps = n_heads * ni
    lspec = pl.BlockSpec((1, dh), lambda h, i: (0, 0))
    cast_in_specs, cast_out_specs, cast_out_shapes = [], [], []
    for w, layer in cast:
        rows, cols = w.shape[1] // n_steps, w.shape[2]
        cast_in_specs.append(
            pl.BlockSpec((None, rows, cols), lambda h, i, layer=layer: (layer, h * ni + i, 0)))
        cast_out_specs.append(pl.BlockSpec((rows, cols), lambda h, i: (h * ni + i, 0)))
        cast_out_shapes.append(jax.ShapeDtypeStruct(w.shape[1:], BF16))
    outs = pl.pallas_call(
        functools.partial(_attn_kernel, ks=ks, depth=depth, n_cast=len(cast)),
        grid=(n_heads, ni),
        in_specs=[
            pl.BlockSpec(memory_space=pltpu.SMEM),
            pl.BlockSpec((HEAD_W, tq), lambda h, i: (h, i)),
            pl.BlockSpec((s, HEAD_W), lambda h, i: (0, h)),
            pl.BlockSpec((s // tk, VT_ROWS, tk), lambda h, i: (0, h, 0)),
            lspec, lspec, lspec, lspec,
            pl.BlockSpec((HEAD_W, 1), lambda h, i: (0, 0)),
        ] + cast_in_specs,
        out_specs=[pl.BlockSpec((tq, HEAD_W), lambda h, i: (i, h))] + cast_out_specs,
        out_shape=[jax.ShapeDtypeStruct((s, n_heads * HEAD_W), BF16)] + cast_out_shapes,
        scratch_shapes=[
            pltpu.VMEM((HEAD_W, 2 * tq), BF16),
            pltpu.VMEM((1, 2 * tq), F32),
            pltpu.VMEM((VT_ROWS, 2 * tq), F32),
            pltpu.VMEM((depth, ks, tk), F32),
        ],
        compiler_params=_params("arbitrary", "arbitrary"),
        name="diff_attn",
    )(lam_init, qt, k, vt, lq1.reshape(1, dh), lk1.reshape(1, dh),
      lq2.reshape(1, dh), lk2.reshape(1, dh), subln_w.reshape(HEAD_W, 1),
      *[w for w, _ in cast])
    return outs[0], outs[1:]


def _out_kernel(x_ref, a_ref, c_ref, wa_ref, wc_ref, o_ref):
    o_ref[...] = (x_ref[...]
                  + jnp.dot(a_ref[...], wa_ref[...], preferred_element_type=F32)
                  + jnp.dot(c_ref[...], wc_ref[...], preferred_element_type=F32))


def _out_proj(x, attn, conv, w_out, layer, *, tm=512):
    s, d = x.shape
    aw, cw = attn.shape[1], conv.shape[1]
    tm = min(tm, s)
    assert s % tm == 0 and aw == cw
    return pl.pallas_call(
        _out_kernel,
        grid=(s // tm,),
        in_specs=[
            pl.BlockSpec((tm, d), lambda i: (i, 0)),
            pl.BlockSpec((tm, aw), lambda i: (i, 0)),
            pl.BlockSpec((tm, cw), lambda i: (i, 0)),
            pl.BlockSpec((None, aw, d), lambda i: (layer, 0, 0)),
            pl.BlockSpec((None, cw, d), lambda i: (layer, 1, 0)),
        ],
        out_specs=pl.BlockSpec((tm, d), lambda i: (i, 0)),
        out_shape=jax.ShapeDtypeStruct((s, d), F32),
        compiler_params=_params("parallel"),
        name="out_proj",
    )(x, attn, conv, w_out, w_out)


def _rope_lane_tables(seq):
    half = ROT_DIM // 2
    pos = jnp.arange(seq, dtype=F32)
    inv_freq = ROPE_THETA ** (-jnp.arange(0, ROT_DIM, 2, dtype=F32) / ROT_DIM)
    ang = pos[:, None] * inv_freq[None, :]
    cos, sin = jnp.cos(ang), jnp.sin(ang)
    rest = DIFF_HEAD_DIM - ROT_DIM
    ones = jnp.ones((seq, rest), F32)
    c = jnp.concatenate([cos, cos, ones], axis=-1)
    s_up = jnp.concatenate([-sin, jnp.zeros((seq, half + rest), F32)], axis=-1)
    s_dn = jnp.concatenate([jnp.zeros((seq, half), F32), sin, jnp.zeros((seq, rest), F32)], axis=-1)
    rep = LANES // DIFF_HEAD_DIM
    return jnp.tile(c, (1, rep)), jnp.tile(s_up, (1, rep)), jnp.tile(s_dn, (1, rep))


def kernel(x, norm_ffn1, ffn1_w_gate, ffn1_w_up, ffn1_w_down, norm_mix, w_in, conv_w,
           lambda_q1, lambda_k1, lambda_q2, lambda_k2, subln_w, w_out,
           norm_ffn2, ffn2_w_gate, ffn2_w_up, ffn2_w_down, norm_final):
    b, s, d = x.shape
    depth = norm_ffn1.shape[0]
    attn_w = d // 2
    n_heads = attn_w // HEAD_W
    rope_c, rope_s1, rope_s2 = _rope_lane_tables(s)
    tm_in = min(512, s)
    n_steps = n_heads * (s // _attn_tq(s, tm_in))
    ffn1 = (ffn1_w_gate, ffn1_w_up, ffn1_w_down)
    ffn2 = (ffn2_w_gate, ffn2_w_up, ffn2_w_down)

    def bf_layer(w, l):
        return w[l].astype(BF16)[None]

    outs = []
    for bi in range(b):
        xb = x[bi]
        cur = {"ffn1": [bf_layer(w, 0) for w in ffn1], "w_in": bf_layer(w_in, 0),
               "w_out": bf_layer(w_out, 0)}
        for l in range(depth):
            lam_init = jnp.full((1,), 0.8 - 0.6 * math.exp(-0.3 * l), F32)
            xb = _ffn(xb, norm_ffn1[l], *cur["ffn1"], 0)
            qt, k, vt, conv = _inproj(xb, norm_mix[l], cur["w_in"], 0, rope_c, rope_s1, rope_s2,
                                      conv_w[l], attn_w, tm=tm_in)
            wanted = [(w, l) for w in ffn2]
            if l + 1 < depth:
                wanted += [(w, l + 1) for w in ffn1] + [(w_in, l + 1), (w_out, l + 1)]
            in_kernel = [(w, ll) for w, ll in wanted if _castable(w, n_steps)]
            attn, casted = _attention(qt, k, vt, lam_init, lambda_q1[l], lambda_k1[l],
                                      lambda_q2[l], lambda_k2[l], subln_w[l], n_heads, in_kernel)
            casted = list(casted)
            ready = [casted.pop(0)[None] if _castable(w, n_steps) else bf_layer(w, ll)
                     for w, ll in wanted]
            xb = _out_proj(xb, attn, conv, cur["w_out"], 0)
            g_final = norm_final if l == depth - 1 else None
            xb = _ffn(xb, norm_ffn2[l], *ready[:3], 0, g_final)
            if l + 1 < depth:
                cur = {"ffn1": ready[3:6], "w_in": ready[6], "w_out": ready[7]}
        outs.append(xb)
    return outs[0][None] if b == 1 else jnp.stack(outs, axis=0)
```

```python
import functools
import math

import jax
import jax.numpy as jnp
from jax import lax
from jax.experimental import pallas as pl
from jax.experimental.pallas import tpu as pltpu

F32 = jnp.float32
BF16 = jnp.bfloat16

EPS = 1e-5
NEG_INF = -1e30
ROPE_THETA = 500000.0
DIFF_HEAD_DIM = 64
HEAD_W = 2 * DIFF_HEAD_DIM
ROT_DIM = DIFF_HEAD_DIM // 4
CONV_K = 3
LOG2E = math.log2(math.e)

VT_PAD = 16
VT_ROWS = HEAD_W + VT_PAD
LANES = 128
SUBLANES = 8
VMEM_LIMIT = 56 * 1024 * 1024

FFN_ROWS, FFN_COLS = 1024, 512
NORM_ROWS = 256
INPROJ_ROWS, INPROJ_COLS = 512, 512
OUTPROJ_ROWS = 512
ATTN_Q = 2048
ATTN_UNIT_KEYS, ATTN_UNIT_COLS = 256, 256
ATTN_DEPTH = 4


def _params(*sem):
    return pltpu.CompilerParams(dimension_semantics=sem, vmem_limit_bytes=VMEM_LIMIT)


def _rms(x, g):
    ms = jnp.mean(x * x, axis=-1, keepdims=True)
    return x * lax.rsqrt(ms + EPS) * g


def _ffn_kernel(*refs, final_norm, rows):
    if final_norm:
        x_ref, g_ref, wg_ref, wu_ref, wd_ref, gf_ref, o_ref, h_ref = refs
    else:
        x_ref, g_ref, wg_ref, wu_ref, wd_ref, o_ref, h_ref = refs
    f = pl.program_id(1)
    tm = x_ref.shape[0]

    @pl.when(f == 0)
    def _():
        def norm_rows(r, carry):
            rs = pl.ds(pl.multiple_of(r * rows, rows), rows)
            h_ref[rs, :] = _rms(x_ref[rs, :], g_ref[...]).astype(BF16)
            return carry

        lax.fori_loop(0, tm // rows, norm_rows, 0)
        o_ref[...] = jnp.zeros_like(o_ref)

    h = h_ref[...]
    gate = jnp.dot(h, wg_ref[...], preferred_element_type=F32)
    up = jnp.dot(h, wu_ref[...], preferred_element_type=F32)
    act = (gate * (1.0 / (1.0 + jnp.exp(-gate))) * up).astype(BF16)
    o_ref[...] += jnp.dot(act, wd_ref[...], preferred_element_type=F32)

    @pl.when(f == pl.num_programs(1) - 1)
    def _():
        def finish_rows(r, carry):
            rs = pl.ds(pl.multiple_of(r * rows, rows), rows)
            y = x_ref[rs, :] + 0.5 * o_ref[rs, :]
            if final_norm:
                y = _rms(y, gf_ref[...])
            o_ref[rs, :] = y
            return carry

        lax.fori_loop(0, tm // rows, finish_rows, 0)


def _ffn(x, g, wg, wu, wd, layer, g_final=None, *, tm=FFN_ROWS, tf=FFN_COLS):
    s, d = x.shape
    dff = wg.shape[2]
    tm, tf = min(tm, s), min(tf, dff)
    assert s % tm == 0 and dff % tf == 0
    final_norm = g_final is not None
    in_specs = [
        pl.BlockSpec((tm, d), lambda i, f: (i, 0)),
        pl.BlockSpec((1, d), lambda i, f: (0, 0)),
        pl.BlockSpec((None, d, tf), lambda i, f: (layer, 0, f)),
        pl.BlockSpec((None, d, tf), lambda i, f: (layer, 0, f)),
        pl.BlockSpec((None, tf, d), lambda i, f: (layer, f, 0)),
    ]
    args = [x, g.reshape(1, d), wg, wu, wd]
    if final_norm:
        in_specs.append(pl.BlockSpec((1, d), lambda i, f: (0, 0)))
        args.append(g_final.reshape(1, d))
    return pl.pallas_call(
        functools.partial(_ffn_kernel, final_norm=final_norm, rows=min(NORM_ROWS, tm)),
        grid=(s // tm, dff // tf),
        in_specs=in_specs,
        out_specs=pl.BlockSpec((tm, d), lambda i, f: (i, 0)),
        out_shape=jax.ShapeDtypeStruct((s, d), F32),
        scratch_shapes=[pltpu.VMEM((tm, d), BF16)],
        compiler_params=_params("parallel", "arbitrary"),
        name="ffn_final" if final_norm else "ffn",
    )(*args)


def _inproj_kernel(x_ref, g_ref, w_ref, c_ref, s1_ref, s2_ref, cw_ref,
                   qt_ref, k_ref, vt_ref, conv_ref, h_ref, z_ref, *, q_scale, attn_w, tn):
    i = pl.program_id(0)
    tm = x_ref.shape[0]
    conv_w = conv_ref.shape[1]

    @pl.when(i == 0)
    def _():
        z_ref[0:SUBLANES, :] = jnp.zeros((SUBLANES, conv_w), F32)

    h_ref[...] = _rms(x_ref[...], g_ref[...]).astype(BF16)
    hn = h_ref[...]

    def proj(col):
        return jnp.dot(hn, w_ref[:, col:col + tn], preferred_element_type=F32)

    def rope(t):
        up = pltpu.roll(t, LANES - ROT_DIM // 2, axis=1)
        dn = pltpu.roll(t, ROT_DIM // 2, axis=1)
        return t * c_ref[...] + up * s1_ref[...] + dn * s2_ref[...]

    ones = jnp.ones((VT_PAD, tm), BF16)
    for n in range(3 * attn_w // tn):
        acc = proj(n * tn)
        kind, base = divmod(n * tn, attn_w)
        for c in range(tn // HEAD_W):
            t = acc[:, c * HEAD_W:(c + 1) * HEAD_W]
            head = base // HEAD_W + c
            if kind == 0:
                qt_ref[head * HEAD_W:(head + 1) * HEAD_W, :] = (rope(t) * q_scale).T.astype(BF16)
            elif kind == 1:
                k_ref[:, head * HEAD_W:(head + 1) * HEAD_W] = rope(t).astype(BF16)
            else:
                r0 = head * VT_ROWS
                vt_ref[r0:r0 + HEAD_W, :] = t.T.astype(BF16)
                vt_ref[r0 + HEAD_W:r0 + VT_ROWS, :] = ones

    for n in range(conv_w // tn):
        cs = slice(n * tn, (n + 1) * tn)
        gate_b = proj(3 * attn_w + n * tn)
        z = proj(3 * attn_w + conv_w + n * tn) * proj(3 * attn_w + 2 * conv_w + n * tn)
        z_ref[SUBLANES:SUBLANES + tm, cs] = z
        z1 = z_ref[SUBLANES - 1:SUBLANES - 1 + tm, cs]
        z2 = z_ref[SUBLANES - 2:SUBLANES - 2 + tm, cs]
        cw = cw_ref[:, cs]
        y = cw[0:1, :] * z2 + cw[1:2, :] * z1 + cw[2:3, :] * z
        conv_ref[:, cs] = (gate_b * y).astype(BF16)
        z_ref[0:SUBLANES, cs] = z_ref[tm:tm + SUBLANES, cs]


def _inproj(x, g, w_in, layer, rope_c, rope_s1, rope_s2, conv_w, attn_w, *, tm, tn=INPROJ_COLS):
    s, d = x.shape
    conv_width = conv_w.shape[1]
    tm = min(tm, s)
    tn = min(tn, attn_w, conv_width)
    assert s % tm == 0 and attn_w % tn == 0 and conv_width % tn == 0
    assert tn % HEAD_W == 0 and HEAD_W == LANES
    assert w_in.shape[2] == 3 * attn_w + 3 * conv_width
    q_scale = DIFF_HEAD_DIM ** -0.5 * LOG2E
    n_heads = attn_w // HEAD_W
    tab = pl.BlockSpec((tm, LANES), lambda i: (i, 0))
    return pl.pallas_call(
        functools.partial(_inproj_kernel, q_scale=q_scale, attn_w=attn_w, tn=tn),
        grid=(s // tm,),
        in_specs=[
            pl.BlockSpec((tm, d), lambda i: (i, 0)),
            pl.BlockSpec((1, d), lambda i: (0, 0)),
            pl.BlockSpec((None, d, w_in.shape[2]), lambda i: (layer, 0, 0),
                         pipeline_mode=pl.Buffered(1)),
            tab, tab, tab,
            pl.BlockSpec((CONV_K, conv_width), lambda i: (0, 0)),
        ],
        out_specs=[
            pl.BlockSpec((attn_w, tm), lambda i: (0, i)),
            pl.BlockSpec((tm, attn_w), lambda i: (i, 0)),
            pl.BlockSpec((None, n_heads * VT_ROWS, tm), lambda i: (i, 0, 0)),
            pl.BlockSpec((tm, conv_width), lambda i: (i, 0)),
        ],
        out_shape=[
            jax.ShapeDtypeStruct((attn_w, s), BF16),
            jax.ShapeDtypeStruct((s, attn_w), BF16),
            jax.ShapeDtypeStruct((s // tm, n_heads * VT_ROWS, tm), BF16),
            jax.ShapeDtypeStruct((s, conv_width), BF16),
        ],
        scratch_shapes=[
            pltpu.VMEM((tm, d), BF16),
            pltpu.VMEM((tm + SUBLANES, conv_width), F32),
        ],
        compiler_params=_params("arbitrary"),
        name="in_proj",
    )(x, g.reshape(1, d), w_in, rope_c, rope_s1, rope_s2, conv_w)


def _attn_kernel(*refs, ks, cw, depth, n_cast):
    (li_ref, qt_ref, k_ref, vt_ref, lq1_ref, lk1_ref, lq2_ref, lk2_ref, w_ref) = refs[:9]
    cast_in = refs[9:9 + n_cast]
    o_ref = refs[9 + n_cast]
    cast_out = refs[10 + n_cast:10 + 2 * n_cast]
    qs_ref, m_ref, acc_ref, st_ref = refs[10 + 2 * n_cast:]
    i = pl.program_id(1)

    for src, dst in zip(cast_in, cast_out):
        dst[...] = src[...].astype(BF16)

    tq = qt_ref.shape[1]
    tk = vt_ref.shape[2]

    qt = qt_ref[...]
    row = lax.broadcasted_iota(jnp.int32, qt.shape, 0)
    zero = jnp.zeros_like(qt)
    qs_ref[:, 0:tq] = jnp.where(row < DIFF_HEAD_DIM, qt, zero)
    qs_ref[:, tq:2 * tq] = jnp.where(row >= DIFF_HEAD_DIM, qt, zero)
    m_ref[...] = jnp.full(m_ref.shape, NEG_INF, F32)
    acc_ref[...] = jnp.zeros(acc_ref.shape, F32)

    nq = tq // tk
    all_units = [(t, c) for t in range(tk // ks) for c in range(2 * tq // cw)]

    def col_pos(c):
        q0 = (c * cw) % tq
        return q0 // tk, q0 % tk

    def diag_units(d):
        keep = []
        for (t, c) in all_units:
            qb, co = col_pos(c)
            if qb > d or (qb == d and t * ks < co + cw):
                keep.append((t, c))
        return keep

    def scores(j, t, c):
        k = k_ref[pl.ds(pl.multiple_of(j * tk + t * ks, ks), ks), :]
        return jnp.dot(k, qs_ref[:, c * cw:(c + 1) * cw], preferred_element_type=F32)

    def softmax_pv(j, t, c, st, diag):
        cs = slice(c * cw, (c + 1) * cw)
        qb, co = col_pos(c)
        if diag is not None and qb == diag and (t + 1) * ks - 1 > co:
            key = lax.broadcasted_iota(jnp.int32, (ks, cw), 0) + t * ks
            qry = lax.broadcasted_iota(jnp.int32, (ks, cw), 1) + co
            st = jnp.where(key <= qry, st, NEG_INF)
        m_prev = m_ref[:, cs]
        m_new = jnp.maximum(m_prev, jnp.max(st, axis=0, keepdims=True))
        alpha = jnp.exp2(m_prev - m_new)
        pt = jnp.exp2(st - m_new)
        vt = vt_ref[j, :, t * ks:(t + 1) * ks]
        acc_ref[:, cs] = alpha * acc_ref[:, cs] + jnp.dot(
            vt, pt.astype(BF16), preferred_element_type=F32)
        m_ref[:, cs] = m_new

    def step(j, units, diag, next_units):
        pending = [st_ref[d] for d in range(depth)]
        for n, (dj, t, c) in enumerate(units):
            st = pending.pop(0)
            ahead = n + depth
            if ahead < len(units):
                aj, at, ac = units[ahead]
                pending.append(scores(j + aj, at, ac))
            elif next_units is not None:
                aj, at, ac = next_units[ahead - len(units)]
                st_ref[ahead - len(units)] = scores(j + aj, at, ac)
            softmax_pv(j + dj, t, c, st, diag)

    def at_slab(dj, units):
        return [(dj, t, c) for (t, c) in units]

    per = 2 if nq % 2 == 0 else 1
    loop_units = [u for dj in range(per) for u in at_slab(dj, all_units)]

    def body(jj, carry):
        step(jj * per, loop_units, None, at_slab(per, all_units))
        return carry

    for d in range(depth):
        st_ref[d] = scores(0, *all_units[d])
    lax.fori_loop(0, i * nq // per, body, 0)
    for d in range(nq):
        nxt = at_slab(1, diag_units(d + 1)) if d + 1 < nq else None
        step(i * nq + d, at_slab(0, diag_units(d)), d, nxt)

    lam = (jnp.exp(jnp.sum(lq1_ref[...] * lk1_ref[...], axis=-1, keepdims=True))
           - jnp.exp(jnp.sum(lq2_ref[...] * lk2_ref[...], axis=-1, keepdims=True))
           + li_ref[0])
    inv_l = 1.0 / acc_ref[HEAD_W:HEAD_W + 1, :]
    o1 = acc_ref[0:HEAD_W, 0:tq] * inv_l[:, 0:tq]
    o2 = acc_ref[0:HEAD_W, tq:2 * tq] * inv_l[:, tq:2 * tq]
    ot = o1 - lam * o2
    ms = jnp.mean(ot * ot, axis=0, keepdims=True)
    ot = ot * lax.rsqrt(ms + EPS) * w_ref[...] * (1.0 - li_ref[0])
    o_ref[...] = ot.T.astype(BF16)


def _attn_tq(s, tk, tq=ATTN_Q):
    return max(min(tq, s), tk)


def _castable(w, n_steps):
    rows = w.shape[1]
    return rows % n_steps == 0 and (rows // n_steps) % 16 == 0


def _attention(qt, k, vt, lam_init, lq1, lk1, lq2, lk2, subln_w, n_heads, cast=(), *,
               ks=ATTN_UNIT_KEYS, cw=ATTN_UNIT_COLS, depth=ATTN_DEPTH):
    s = k.shape[0]
    tk = vt.shape[2]
    tq, ks, cw = _attn_tq(s, tk), min(ks, tk), min(cw, tk)
    assert s % tq == 0 and tq % tk == 0 and tk % ks == 0
    dh = DIFF_HEAD_DIM
    ni = s // tq
    n_steps = n_heads * ni
    lspec = pl.BlockSpec((1, dh), lambda h, i: (0, 0))
    cast_in_specs, cast_out_specs, cast_out_shapes = [], [], []
    for w, layer in cast:
        rows, cols = w.shape[1] // n_steps, w.shape[2]
        cast_in_specs.append(
            pl.BlockSpec((None, rows, cols), lambda h, i, layer=layer: (layer, h * ni + i, 0)))
        cast_out_specs.append(pl.BlockSpec((rows, cols), lambda h, i: (h * ni + i, 0)))
        cast_out_shapes.append(jax.ShapeDtypeStruct(w.shape[1:], BF16))
    outs = pl.pallas_call(
        functools.partial(_attn_kernel, ks=ks, cw=cw, depth=depth, n_cast=len(cast)),
        grid=(n_heads, ni),
        in_specs=[
            pl.BlockSpec(memory_space=pltpu.SMEM),
            pl.BlockSpec((HEAD_W, tq), lambda h, i: (h, i)),
            pl.BlockSpec((s, HEAD_W), lambda h, i: (0, h)),
            pl.BlockSpec((s // tk, VT_ROWS, tk), lambda h, i: (0, h, 0)),
            lspec, lspec, lspec, lspec,
            pl.BlockSpec((HEAD_W, 1), lambda h, i: (0, 0)),
        ] + cast_in_specs,
        out_specs=[pl.BlockSpec((tq, HEAD_W), lambda h, i: (i, h))] + cast_out_specs,
        out_shape=[jax.ShapeDtypeStruct((s, n_heads * HEAD_W), BF16)] + cast_out_shapes,
        scratch_shapes=[
            pltpu.VMEM((HEAD_W, 2 * tq), BF16),
            pltpu.VMEM((1, 2 * tq), F32),
            pltpu.VMEM((VT_ROWS, 2 * tq), F32),
            pltpu.VMEM((depth, ks, cw), F32),
        ],
        compiler_params=_params("arbitrary", "arbitrary"),
        name="diff_attn",
    )(lam_init, qt, k, vt, lq1.reshape(1, dh), lk1.reshape(1, dh),
      lq2.reshape(1, dh), lk2.reshape(1, dh), subln_w.reshape(HEAD_W, 1),
      *[w for w, _ in cast])
    return outs[0], outs[1:]


def _out_kernel(x_ref, a_ref, c_ref, wa_ref, wc_ref, o_ref):
    o_ref[...] = (x_ref[...]
                  + jnp.dot(a_ref[...], wa_ref[...], preferred_element_type=F32)
                  + jnp.dot(c_ref[...], wc_ref[...], preferred_element_type=F32))


def _out_proj(x, attn, conv, w_out, layer, *, tm=OUTPROJ_ROWS):
    s, d = x.shape
    aw, cw = attn.shape[1], conv.shape[1]
    tm = min(tm, s)
    assert s % tm == 0 and aw == cw
    return pl.pallas_call(
        _out_kernel,
        grid=(s // tm,),
        in_specs=[
            pl.BlockSpec((tm, d), lambda i: (i, 0)),
            pl.BlockSpec((tm, aw), lambda i: (i, 0)),
            pl.BlockSpec((tm, cw), lambda i: (i, 0)),
            pl.BlockSpec((None, aw, d), lambda i: (layer, 0, 0)),
            pl.BlockSpec((None, cw, d), lambda i: (layer, 1, 0)),
        ],
        out_specs=pl.BlockSpec((tm, d), lambda i: (i, 0)),
        out_shape=jax.ShapeDtypeStruct((s, d), F32),
        compiler_params=_params("parallel"),
        name="out_proj",
    )(x, attn, conv, w_out, w_out)


def _rope_lane_tables(seq):
    half = ROT_DIM // 2
    pos = jnp.arange(seq, dtype=F32)
    inv_freq = ROPE_THETA ** (-jnp.arange(0, ROT_DIM, 2, dtype=F32) / ROT_DIM)
    ang = pos[:, None] * inv_freq[None, :]
    cos, sin = jnp.cos(ang), jnp.sin(ang)
    rest = DIFF_HEAD_DIM - ROT_DIM
    ones = jnp.ones((seq, rest), F32)
    c = jnp.concatenate([cos, cos, ones], axis=-1)
    s_up = jnp.concatenate([-sin, jnp.zeros((seq, half + rest), F32)], axis=-1)
    s_dn = jnp.concatenate([jnp.zeros((seq, half), F32), sin, jnp.zeros((seq, rest), F32)], axis=-1)
    rep = LANES // DIFF_HEAD_DIM
    return jnp.tile(c, (1, rep)), jnp.tile(s_up, (1, rep)), jnp.tile(s_dn, (1, rep))


def kernel(x, norm_ffn1, ffn1_w_gate, ffn1_w_up, ffn1_w_down, norm_mix, w_in, conv_w,
           lambda_q1, lambda_k1, lambda_q2, lambda_k2, subln_w, w_out,
           norm_ffn2, ffn2_w_gate, ffn2_w_up, ffn2_w_down, norm_final):
    b, s, d = x.shape
    depth = norm_ffn1.shape[0]
    attn_w = d // 2
    n_heads = attn_w // HEAD_W
    rope_c, rope_s1, rope_s2 = _rope_lane_tables(s)
    tm_in = min(INPROJ_ROWS, s)
    n_steps = n_heads * (s // _attn_tq(s, tm_in))
    ffn1 = (ffn1_w_gate, ffn1_w_up, ffn1_w_down)
    ffn2 = (ffn2_w_gate, ffn2_w_up, ffn2_w_down)

    def bf_layer(w, l):
        return w[l].astype(BF16)[None]

    outs = []
    for bi in range(b):
        xb = x[bi]
        cur = {"ffn1": [bf_layer(w, 0) for w in ffn1], "w_in": bf_layer(w_in, 0),
               "w_out": bf_layer(w_out, 0)}
        for l in range(depth):
            lam_init = jnp.full((1,), 0.8 - 0.6 * math.exp(-0.3 * l), F32)
            xb = _ffn(xb, norm_ffn1[l], *cur["ffn1"], 0)
            qt, k, vt, conv = _inproj(xb, norm_mix[l], cur["w_in"], 0, rope_c, rope_s1, rope_s2,
                                      conv_w[l], attn_w, tm=tm_in)
            wanted = [(w, l) for w in ffn2]
            if l + 1 < depth:
                wanted += [(w, l + 1) for w in ffn1] + [(w_in, l + 1), (w_out, l + 1)]
            in_kernel = [(w, ll) for w, ll in wanted if _castable(w, n_steps)]
            attn, casted = _attention(qt, k, vt, lam_init, lambda_q1[l], lambda_k1[l],
                                      lambda_q2[l], lambda_k2[l], subln_w[l], n_heads, in_kernel)
            casted = list(casted)
            ready = [casted.pop(0)[None] if _castable(w, n_steps) else bf_layer(w, ll)
                     for w, ll in wanted]
            xb = _out_proj(xb, attn, conv, cur["w_out"], 0)
            g_final = norm_final if l == depth - 1 else None
            xb = _ffn(xb, norm_ffn2[l], *ready[:3], 0, g_final)
            if l + 1 < depth:
                cur = {"ffn1": ready[3:6], "w_in": ready[6], "w_out": ready[7]}
        outs.append(xb)
    return outs[0][None] if b == 1 else jnp.stack(outs, axis=0)
```

```python
import functools
import math

import jax
import jax.numpy as jnp
from jax import lax
from jax.experimental import pallas as pl
from jax.experimental.pallas import tpu as pltpu

F32 = jnp.float32
BF16 = jnp.bfloat16

EPS = 1e-5
NEG_INF = -1e30
ROPE_THETA = 500000.0
DIFF_HEAD_DIM = 64
HEAD_W = 2 * DIFF_HEAD_DIM
ROT_DIM = DIFF_HEAD_DIM // 4
CONV_K = 3
LOG2E = math.log2(math.e)

EXP2_HEADROOM = 60.0
VT_PAD = 16
VT_ROWS = HEAD_W + VT_PAD
LANES = 128
SUBLANES = 8
VMEM_LIMIT = 56 * 1024 * 1024

FFN_ROWS, FFN_COLS = 1024, 512
NORM_ROWS = 256
INPROJ_ROWS, INPROJ_COLS = 512, 512
OUTPROJ_ROWS = 512
ATTN_Q = 2048
ATTN_UNIT_KEYS, ATTN_UNIT_COLS = 256, 256
ATTN_DEPTH = 4


def _params(*sem):
    return pltpu.CompilerParams(dimension_semantics=sem, vmem_limit_bytes=VMEM_LIMIT)


def _rms(x, g):
    ms = jnp.mean(x * x, axis=-1, keepdims=True)
    return x * lax.rsqrt(ms + EPS) * g


def _ffn_kernel(*refs, final_norm, rows):
    if final_norm:
        x_ref, g_ref, wg_ref, wu_ref, wd_ref, gf_ref, o_ref, h_ref = refs
    else:
        x_ref, g_ref, wg_ref, wu_ref, wd_ref, o_ref, h_ref = refs
    f = pl.program_id(1)
    tm = x_ref.shape[0]

    @pl.when(f == 0)
    def _():
        def norm_rows(r, carry):
            rs = pl.ds(pl.multiple_of(r * rows, rows), rows)
            h_ref[rs, :] = _rms(x_ref[rs, :], g_ref[...]).astype(BF16)
            return carry

        lax.fori_loop(0, tm // rows, norm_rows, 0)
        o_ref[...] = jnp.zeros_like(o_ref)

    h = h_ref[...]
    gate = jnp.dot(h, wg_ref[...], preferred_element_type=F32)
    up = jnp.dot(h, wu_ref[...], preferred_element_type=F32)
    act = (gate * (1.0 / (1.0 + jnp.exp(-gate))) * up).astype(BF16)
    o_ref[...] += jnp.dot(act, wd_ref[...], preferred_element_type=F32)

    @pl.when(f == pl.num_programs(1) - 1)
    def _():
        def finish_rows(r, carry):
            rs = pl.ds(pl.multiple_of(r * rows, rows), rows)
            y = x_ref[rs, :] + 0.5 * o_ref[rs, :]
            if final_norm:
                y = _rms(y, gf_ref[...])
            o_ref[rs, :] = y
            return carry

        lax.fori_loop(0, tm // rows, finish_rows, 0)


def _ffn(x, g, wg, wu, wd, layer, g_final=None, *, tm=FFN_ROWS, tf=FFN_COLS):
    s, d = x.shape
    dff = wg.shape[2]
    tm, tf = min(tm, s), min(tf, dff)
    assert s % tm == 0 and dff % tf == 0
    final_norm = g_final is not None
    in_specs = [
        pl.BlockSpec((tm, d), lambda i, f: (i, 0)),
        pl.BlockSpec((1, d), lambda i, f: (0, 0)),
        pl.BlockSpec((None, d, tf), lambda i, f: (layer, 0, f)),
        pl.BlockSpec((None, d, tf), lambda i, f: (layer, 0, f)),
        pl.BlockSpec((None, tf, d), lambda i, f: (layer, f, 0)),
    ]
    args = [x, g.reshape(1, d), wg, wu, wd]
    if final_norm:
        in_specs.append(pl.BlockSpec((1, d), lambda i, f: (0, 0)))
        args.append(g_final.reshape(1, d))
    return pl.pallas_call(
        functools.partial(_ffn_kernel, final_norm=final_norm, rows=min(NORM_ROWS, tm)),
        grid=(s // tm, dff // tf),
        in_specs=in_specs,
        out_specs=pl.BlockSpec((tm, d), lambda i, f: (i, 0)),
        out_shape=jax.ShapeDtypeStruct((s, d), F32),
        scratch_shapes=[pltpu.VMEM((tm, d), BF16)],
        compiler_params=_params("parallel", "arbitrary"),
        name="ffn_final" if final_norm else "ffn",
    )(*args)


def _inproj_kernel(x_ref, g_ref, w_ref, c_ref, s1_ref, s2_ref, cw_ref,
                   qt_ref, k_ref, vt_ref, conv_ref, h_ref, z_ref, *, q_scale, attn_w, tn):
    i = pl.program_id(0)
    tm = x_ref.shape[0]
    conv_w = conv_ref.shape[1]

    @pl.when(i == 0)
    def _():
        z_ref[0:SUBLANES, :] = jnp.zeros((SUBLANES, conv_w), F32)

    h_ref[...] = _rms(x_ref[...], g_ref[...]).astype(BF16)
    hn = h_ref[...]

    def proj(col):
        return jnp.dot(hn, w_ref[:, col:col + tn], preferred_element_type=F32)

    def rope(t):
        up = pltpu.roll(t, LANES - ROT_DIM // 2, axis=1)
        dn = pltpu.roll(t, ROT_DIM // 2, axis=1)
        return t * c_ref[...] + up * s1_ref[...] + dn * s2_ref[...]

    ones = jnp.ones((VT_PAD, tm), BF16)
    for n in range(3 * attn_w // tn):
        acc = proj(n * tn)
        kind, base = divmod(n * tn, attn_w)
        for c in range(tn // HEAD_W):
            t = acc[:, c * HEAD_W:(c + 1) * HEAD_W]
            head = base // HEAD_W + c
            if kind == 0:
                qt_ref[head * HEAD_W:(head + 1) * HEAD_W, :] = (rope(t) * q_scale).T.astype(BF16)
            elif kind == 1:
                k_ref[:, head * HEAD_W:(head + 1) * HEAD_W] = rope(t).astype(BF16)
            else:
                r0 = head * VT_ROWS
                vt_ref[r0:r0 + HEAD_W, :] = t.T.astype(BF16)
                vt_ref[r0 + HEAD_W:r0 + VT_ROWS, :] = ones

    for n in range(conv_w // tn):
        cs = slice(n * tn, (n + 1) * tn)
        gate_b = proj(3 * attn_w + n * tn)
        z = proj(3 * attn_w + conv_w + n * tn) * proj(3 * attn_w + 2 * conv_w + n * tn)
        z_ref[SUBLANES:SUBLANES + tm, cs] = z
        z1 = z_ref[SUBLANES - 1:SUBLANES - 1 + tm, cs]
        z2 = z_ref[SUBLANES - 2:SUBLANES - 2 + tm, cs]
        cw = cw_ref[:, cs]
        y = cw[0:1, :] * z2 + cw[1:2, :] * z1 + cw[2:3, :] * z
        conv_ref[:, cs] = (gate_b * y).astype(BF16)
        z_ref[0:SUBLANES, cs] = z_ref[tm:tm + SUBLANES, cs]


def _inproj(x, g, w_in, layer, rope_c, rope_s1, rope_s2, conv_w, attn_w, *, tm, tn=INPROJ_COLS):
    s, d = x.shape
    conv_width = conv_w.shape[1]
    tm = min(tm, s)
    tn = min(tn, attn_w, conv_width)
    assert s % tm == 0 and attn_w % tn == 0 and conv_width % tn == 0
    assert tn % HEAD_W == 0 and HEAD_W == LANES
    assert w_in.shape[2] == 3 * attn_w + 3 * conv_width
    q_scale = DIFF_HEAD_DIM ** -0.5 * LOG2E
    n_heads = attn_w // HEAD_W
    tab = pl.BlockSpec((tm, LANES), lambda i: (i, 0))
    return pl.pallas_call(
        functools.partial(_inproj_kernel, q_scale=q_scale, attn_w=attn_w, tn=tn),
        grid=(s // tm,),
        in_specs=[
            pl.BlockSpec((tm, d), lambda i: (i, 0)),
            pl.BlockSpec((1, d), lambda i: (0, 0)),
            pl.BlockSpec((None, d, w_in.shape[2]), lambda i: (layer, 0, 0),
                         pipeline_mode=pl.Buffered(1)),
            tab, tab, tab,
            pl.BlockSpec((CONV_K, conv_width), lambda i: (0, 0)),
        ],
        out_specs=[
            pl.BlockSpec((attn_w, tm), lambda i: (0, i)),
            pl.BlockSpec((tm, attn_w), lambda i: (i, 0)),
            pl.BlockSpec((None, n_heads * VT_ROWS, tm), lambda i: (i, 0, 0)),
            pl.BlockSpec((tm, conv_width), lambda i: (i, 0)),
        ],
        out_shape=[
            jax.ShapeDtypeStruct((attn_w, s), BF16),
            jax.ShapeDtypeStruct((s, attn_w), BF16),
            jax.ShapeDtypeStruct((s // tm, n_heads * VT_ROWS, tm), BF16),
            jax.ShapeDtypeStruct((s, conv_width), BF16),
        ],
        scratch_shapes=[
            pltpu.VMEM((tm, d), BF16),
            pltpu.VMEM((tm + SUBLANES, conv_width), F32),
        ],
        compiler_params=_params("arbitrary"),
        name="in_proj",
    )(x, g.reshape(1, d), w_in, rope_c, rope_s1, rope_s2, conv_w)


def _attn_kernel(*refs, ks, cw, depth, n_cast):
    (li_ref, qt_ref, k_ref, vt_ref, lq1_ref, lk1_ref, lq2_ref, lk2_ref, w_ref) = refs[:9]
    cast_in = refs[9:9 + n_cast]
    o_ref = refs[9 + n_cast]
    cast_out = refs[10 + n_cast:10 + 2 * n_cast]
    qs_ref, m_ref, shift_ref, over_ref, acc_ref, st_ref = refs[10 + 2 * n_cast:]
    i = pl.program_id(1)

    for src, dst in zip(cast_in, cast_out):
        dst[...] = src[...].astype(BF16)

    tq = qt_ref.shape[1]
    tk = vt_ref.shape[2]

    qt = qt_ref[...]
    row = lax.broadcasted_iota(jnp.int32, qt.shape, 0)
    zero = jnp.zeros_like(qt)
    qs_ref[:, 0:tq] = jnp.where(row < DIFF_HEAD_DIM, qt, zero)
    qs_ref[:, tq:2 * tq] = jnp.where(row >= DIFF_HEAD_DIM, qt, zero)

    k_own = k_ref[pl.ds(pl.multiple_of(i * tq, tq), tq), :].astype(F32)
    own = qt.astype(F32) * k_own.T
    shift_ref[:, 0:tq] = jnp.sum(own[0:DIFF_HEAD_DIM, :], axis=0, keepdims=True)
    shift_ref[:, tq:2 * tq] = jnp.sum(own[DIFF_HEAD_DIM:, :], axis=0, keepdims=True)

    nq = tq // tk
    all_units = [(t, c) for t in range(tk // ks) for c in range(2 * tq // cw)]

    def col_pos(c):
        q0 = (c * cw) % tq
        return q0 // tk, q0 % tk

    def diag_units(d):
        keep = []
        for (t, c) in all_units:
            qb, co = col_pos(c)
            if qb > d or (qb == d and t * ks < co + cw):
                keep.append((t, c))
        return keep

    def scores(j, t, c):
        k = k_ref[pl.ds(pl.multiple_of(j * tk + t * ks, ks), ks), :]
        return jnp.dot(k, qs_ref[:, c * cw:(c + 1) * cw], preferred_element_type=F32)

    def masked(t, c, st, diag):
        qb, co = col_pos(c)
        if diag is not None and qb == diag and (t + 1) * ks - 1 > co:
            key = lax.broadcasted_iota(jnp.int32, (ks, cw), 0) + t * ks
            qry = lax.broadcasted_iota(jnp.int32, (ks, cw), 1) + co
            st = jnp.where(key <= qry, st, NEG_INF)
        return st

    def pv(j, t, pt):
        vt = vt_ref[j, :, t * ks:(t + 1) * ks]
        return jnp.dot(vt, pt.astype(BF16), preferred_element_type=F32)

    def fixed_shift_unit(j, t, c, st, diag):
        cs = slice(c * cw, (c + 1) * cw)
        x = masked(t, c, st, diag) - shift_ref[:, cs]
        over_ref[:, cs] = jnp.maximum(over_ref[:, cs], jnp.max(x, axis=0, keepdims=True))
        acc_ref[:, cs] += pv(j, t, jnp.exp2(x))

    def running_max_unit(j, t, c, st, diag):
        cs = slice(c * cw, (c + 1) * cw)
        st = masked(t, c, st, diag)
        m_prev = m_ref[:, cs]
        m_new = jnp.maximum(m_prev, jnp.max(st, axis=0, keepdims=True))
        alpha = jnp.exp2(m_prev - m_new)
        acc_ref[:, cs] = alpha * acc_ref[:, cs] + pv(j, t, jnp.exp2(st - m_new))
        m_ref[:, cs] = m_new

    def at_slab(dj, units):
        return [(dj, t, c) for (t, c) in units]

    def attend(unit_fn):
        acc_ref[...] = jnp.zeros(acc_ref.shape, F32)

        def step(j, units, diag, next_units):
            pending = [st_ref[d] for d in range(depth)]
            for n, (dj, t, c) in enumerate(units):
                st = pending.pop(0)
                ahead = n + depth
                if ahead < len(units):
                    aj, at, ac = units[ahead]
                    pending.append(scores(j + aj, at, ac))
                elif next_units is not None:
                    aj, at, ac = next_units[ahead - len(units)]
                    st_ref[ahead - len(units)] = scores(j + aj, at, ac)
                unit_fn(j + dj, t, c, st, diag)

        per = 2 if nq % 2 == 0 else 1
        loop_units = [u for dj in range(per) for u in at_slab(dj, all_units)]

        def body(jj, carry):
            step(jj * per, loop_units, None, at_slab(per, all_units))
            return carry

        for d in range(depth):
            st_ref[d] = scores(0, *all_units[d])
        lax.fori_loop(0, i * nq // per, body, 0)
        for d in range(nq):
            nxt = at_slab(1, diag_units(d + 1)) if d + 1 < nq else None
            step(i * nq + d, at_slab(0, diag_units(d)), d, nxt)

        lam = (jnp.exp(jnp.sum(lq1_ref[...] * lk1_ref[...], axis=-1, keepdims=True))
               - jnp.exp(jnp.sum(lq2_ref[...] * lk2_ref[...], axis=-1, keepdims=True))
               + li_ref[0])
        inv_l = 1.0 / acc_ref[HEAD_W:HEAD_W + 1, :]
        o1 = acc_ref[0:HEAD_W, 0:tq] * inv_l[:, 0:tq]
        o2 = acc_ref[0:HEAD_W, tq:2 * tq] * inv_l[:, tq:2 * tq]
        ot = o1 - lam * o2
        ms = jnp.mean(ot * ot, axis=0, keepdims=True)
        ot = ot * lax.rsqrt(ms + EPS) * w_ref[...] * (1.0 - li_ref[0])
        o_ref[...] = ot.T.astype(BF16)

    over_ref[...] = jnp.full(over_ref.shape, NEG_INF, F32)
    attend(fixed_shift_unit)

    @pl.when(jnp.max(over_ref[...]) > EXP2_HEADROOM)
    def _():
        m_ref[...] = jnp.full(m_ref.shape, NEG_INF, F32)
        attend(running_max_unit)


def _attn_tq(s, tk, tq=ATTN_Q):
    return max(min(tq, s), tk)


def _castable(w, n_steps):
    rows = w.shape[1]
    return rows % n_steps == 0 and (rows // n_steps) % 16 == 0


def _attention(qt, k, vt, lam_init, lq1, lk1, lq2, lk2, subln_w, n_heads, cast=(), *,
               ks=ATTN_UNIT_KEYS, cw=ATTN_UNIT_COLS, depth=ATTN_DEPTH):
    s = k.shape[0]
    tk = vt.shape[2]
    tq, ks, cw = _attn_tq(s, tk), min(ks, tk), min(cw, tk)
    assert s % tq == 0 and tq % tk == 0 and tk % ks == 0
    dh = DIFF_HEAD_DIM
    ni = s // tq
    n_steps = n_heads * ni
    lspec = pl.BlockSpec((1, dh), lambda h, i: (0, 0))
    cast_in_specs, cast_out_specs, cast_out_shapes = [], [], []
    for w, layer in cast:
        rows, cols = w.shape[1] // n_steps, w.shape[2]
        cast_in_specs.append(
            pl.BlockSpec((None, rows, cols), lambda h, i, layer=layer: (layer, h * ni + i, 0)))
        cast_out_specs.append(pl.BlockSpec((rows, cols), lambda h, i: (h * ni + i, 0)))
        cast_out_shapes.append(jax.ShapeDtypeStruct(w.shape[1:], BF16))
    outs = pl.pallas_call(
        functools.partial(_attn_kernel, ks=ks, cw=cw, depth=depth, n_cast=len(cast)),
        grid=(n_heads, ni),
        in_specs=[
            pl.BlockSpec(memory_space=pltpu.SMEM),
            pl.BlockSpec((HEAD_W, tq), lambda h, i: (h, i)),
            pl.BlockSpec((s, HEAD_W), lambda h, i: (0, h)),
            pl.BlockSpec((s // tk, VT_ROWS, tk), lambda h, i: (0, h, 0)),
            lspec, lspec, lspec, lspec,
            pl.BlockSpec((HEAD_W, 1), lambda h, i: (0, 0)),
        ] + cast_in_specs,
        out_specs=[pl.BlockSpec((tq, HEAD_W), lambda h, i: (i, h))] + cast_out_specs,
        out_shape=[jax.ShapeDtypeStruct((s, n_heads * HEAD_W), BF16)] + cast_out_shapes,
        scratch_shapes=[
            pltpu.VMEM((HEAD_W, 2 * tq), BF16),
            pltpu.VMEM((1, 2 * tq), F32),
            pltpu.VMEM((1, 2 * tq), F32),
            pltpu.VMEM((1, 2 * tq), F32),
            pltpu.VMEM((VT_ROWS, 2 * tq), F32),
            pltpu.VMEM((depth, ks, cw), F32),
        ],
        compiler_params=_params("arbitrary", "arbitrary"),
        name="diff_attn",
    )(lam_init, qt, k, vt, lq1.reshape(1, dh), lk1.reshape(1, dh),
      lq2.reshape(1, dh), lk2.reshape(1, dh), subln_w.reshape(HEAD_W, 1),
      *[w for w, _ in cast])
    return outs[0], outs[1:]


def _out_kernel(x_ref, a_ref, c_ref, wa_ref, wc_ref, o_ref):
    o_ref[...] = (x_ref[...]
                  + jnp.dot(a_ref[...], wa_ref[...], preferred_element_type=F32)
                  + jnp.dot(c_ref[...], wc_ref[...], preferred_element_type=F32))


def _out_proj(x, attn, conv, w_out, layer, *, tm=OUTPROJ_ROWS):
    s, d = x.shape
    aw, cw = attn.shape[1], conv.shape[1]
    tm = min(tm, s)
    assert s % tm == 0 and aw == cw
    return pl.pallas_call(
        _out_kernel,
        grid=(s // tm,),
        in_specs=[
            pl.BlockSpec((tm, d), lambda i: (i, 0)),
            pl.BlockSpec((tm, aw), lambda i: (i, 0)),
            pl.BlockSpec((tm, cw), lambda i: (i, 0)),
            pl.BlockSpec((None, aw, d), lambda i: (layer, 0, 0)),
            pl.BlockSpec((None, cw, d), lambda i: (layer, 1, 0)),
        ],
        out_specs=pl.BlockSpec((tm, d), lambda i: (i, 0)),
        out_shape=jax.ShapeDtypeStruct((s, d), F32),
        compiler_params=_params("parallel"),
        name="out_proj",
    )(x, attn, conv, w_out, w_out)


def _rope_lane_tables(seq):
    half = ROT_DIM // 2
    pos = jnp.arange(seq, dtype=F32)
    inv_freq = ROPE_THETA ** (-jnp.arange(0, ROT_DIM, 2, dtype=F32) / ROT_DIM)
    ang = pos[:, None] * inv_freq[None, :]
    cos, sin = jnp.cos(ang), jnp.sin(ang)
    rest = DIFF_HEAD_DIM - ROT_DIM
    ones = jnp.ones((seq, rest), F32)
    c = jnp.concatenate([cos, cos, ones], axis=-1)
    s_up = jnp.concatenate([-sin, jnp.zeros((seq, half + rest), F32)], axis=-1)
    s_dn = jnp.concatenate([jnp.zeros((seq, half), F32), sin, jnp.zeros((seq, rest), F32)], axis=-1)
    rep = LANES // DIFF_HEAD_DIM
    return jnp.tile(c, (1, rep)), jnp.tile(s_up, (1, rep)), jnp.tile(s_dn, (1, rep))


def kernel(x, norm_ffn1, ffn1_w_gate, ffn1_w_up, ffn1_w_down, norm_mix, w_in, conv_w,
           lambda_q1, lambda_k1, lambda_q2, lambda_k2, subln_w, w_out,
           norm_ffn2, ffn2_w_gate, ffn2_w_up, ffn2_w_down, norm_final):
    b, s, d = x.shape
    depth = norm_ffn1.shape[0]
    attn_w = d // 2
    n_heads = attn_w // HEAD_W
    rope_c, rope_s1, rope_s2 = _rope_lane_tables(s)
    tm_in = min(INPROJ_ROWS, s)
    n_steps = n_heads * (s // _attn_tq(s, tm_in))
    ffn1 = (ffn1_w_gate, ffn1_w_up, ffn1_w_down)
    ffn2 = (ffn2_w_gate, ffn2_w_up, ffn2_w_down)

    def bf_layer(w, l):
        return w[l].astype(BF16)[None]

    outs = []
    for bi in range(b):
        xb = x[bi]
        cur = {"ffn1": [bf_layer(w, 0) for w in ffn1], "w_in": bf_layer(w_in, 0),
               "w_out": bf_layer(w_out, 0)}
        for l in range(depth):
            lam_init = jnp.full((1,), 0.8 - 0.6 * math.exp(-0.3 * l), F32)
            xb = _ffn(xb, norm_ffn1[l], *cur["ffn1"], 0)
            qt, k, vt, conv = _inproj(xb, norm_mix[l], cur["w_in"], 0, rope_c, rope_s1, rope_s2,
                                      conv_w[l], attn_w, tm=tm_in)
            wanted = [(w, l) for w in ffn2]
            if l + 1 < depth:
                wanted += [(w, l + 1) for w in ffn1] + [(w_in, l + 1), (w_out, l + 1)]
            in_kernel = [(w, ll) for w, ll in wanted if _castable(w, n_steps)]
            attn, casted = _attention(qt, k, vt, lam_init, lambda_q1[l], lambda_k1[l],
                                      lambda_q2[l], lambda_k2[l], subln_w[l], n_heads, in_kernel)
            casted = list(casted)
            ready = [casted.pop(0)[None] if _castable(w, n_steps) else bf_layer(w, ll)
                     for w, ll in wanted]
            xb = _out_proj(xb, attn, conv, cur["w_out"], 0)
            g_final = norm_final if l == depth - 1 else None
            xb = _ffn(xb, norm_ffn2[l], *ready[:3], 0, g_final)
            if l + 1 < depth:
                cur = {"ffn1": ready[3:6], "w_in": ready[6], "w_out": ready[7]}
        outs.append(xb)
    return outs[0][None] if b == 1 else jnp.stack(outs, axis=0)
```

```python
import functools
import math

import jax
import jax.numpy as jnp
from jax import lax
from jax.experimental import pallas as pl
from jax.experimental.pallas import tpu as pltpu

F32 = jnp.float32
BF16 = jnp.bfloat16

EPS = 1e-5
NEG_INF = -1e30
ROPE_THETA = 500000.0
DIFF_HEAD_DIM = 64
HEAD_W = 2 * DIFF_HEAD_DIM
ROT_DIM = DIFF_HEAD_DIM // 4
CONV_K = 3
LOG2E = math.log2(math.e)

EXP2_HEADROOM = 60.0
VT_PAD = 16
VT_ROWS = HEAD_W + VT_PAD
LANES = 128
SUBLANES = 8
VMEM_LIMIT = 56 * 1024 * 1024

FFN_ROWS, FFN_COLS = 1024, 512
NORM_ROWS = 256
INPROJ_ROWS, INPROJ_COLS = 512, 512
OUTPROJ_ROWS = 512
ATTN_Q = 2048
ATTN_UNIT_KEYS, ATTN_UNIT_COLS = 256, 256
ATTN_DEPTH = 4


def _params(*sem):
    return pltpu.CompilerParams(dimension_semantics=sem, vmem_limit_bytes=VMEM_LIMIT)


def _rms(x, g):
    ms = jnp.mean(x * x, axis=-1, keepdims=True)
    return x * lax.rsqrt(ms + EPS) * g


def _ffn_kernel(*refs, final_norm, rows):
    if final_norm:
        x_ref, g_ref, wg_ref, wu_ref, wd_ref, gf_ref, o_ref, h_ref = refs
    else:
        x_ref, g_ref, wg_ref, wu_ref, wd_ref, o_ref, h_ref = refs
    f = pl.program_id(1)
    tm = x_ref.shape[0]

    @pl.when(f == 0)
    def _():
        def norm_rows(r, carry):
            rs = pl.ds(pl.multiple_of(r * rows, rows), rows)
            h_ref[rs, :] = _rms(x_ref[rs, :], g_ref[...]).astype(BF16)
            return carry

        lax.fori_loop(0, tm // rows, norm_rows, 0)
        o_ref[...] = jnp.zeros_like(o_ref)

    h = h_ref[...]
    gate = jnp.dot(h, wg_ref[...], preferred_element_type=F32)
    up = jnp.dot(h, wu_ref[...], preferred_element_type=F32)
    act = (gate * (1.0 / (1.0 + jnp.exp(-gate))) * up).astype(BF16)
    o_ref[...] += jnp.dot(act, wd_ref[...], preferred_element_type=F32)

    @pl.when(f == pl.num_programs(1) - 1)
    def _():
        def finish_rows(r, carry):
            rs = pl.ds(pl.multiple_of(r * rows, rows), rows)
            y = x_ref[rs, :] + 0.5 * o_ref[rs, :]
            if final_norm:
                y = _rms(y, gf_ref[...])
            o_ref[rs, :] = y
            return carry

        lax.fori_loop(0, tm // rows, finish_rows, 0)


def _ffn(x, g, wg, wu, wd, g_final=None, *, tm=FFN_ROWS):
    s, d = x.shape
    nf, _, tf = wg.shape
    dff = nf * tf
    tm = min(tm, s)
    assert s % tm == 0 and wd.shape == (dff, d)
    final_norm = g_final is not None
    in_specs = [
        pl.BlockSpec((tm, d), lambda i, f: (i, 0)),
        pl.BlockSpec((1, d), lambda i, f: (0, 0)),
        pl.BlockSpec((None, d, tf), lambda i, f: (f, 0, 0)),
        pl.BlockSpec((None, d, tf), lambda i, f: (f, 0, 0)),
        pl.BlockSpec((tf, d), lambda i, f: (f, 0)),
    ]
    args = [x, g.reshape(1, d), wg, wu, wd]
    if final_norm:
        in_specs.append(pl.BlockSpec((1, d), lambda i, f: (0, 0)))
        args.append(g_final.reshape(1, d))
    return pl.pallas_call(
        functools.partial(_ffn_kernel, final_norm=final_norm, rows=min(NORM_ROWS, tm)),
        grid=(s // tm, nf),
        in_specs=in_specs,
        out_specs=pl.BlockSpec((tm, d), lambda i, f: (i, 0)),
        out_shape=jax.ShapeDtypeStruct((s, d), F32),
        scratch_shapes=[pltpu.VMEM((tm, d), BF16)],
        compiler_params=_params("parallel", "arbitrary"),
        name="ffn_final" if final_norm else "ffn",
    )(*args)


def _inproj_kernel(x_ref, g_ref, w_ref, c_ref, s1_ref, s2_ref, cw_ref,
                   qt_ref, k_ref, vt_ref, conv_ref, h_ref, z_ref, *, q_scale, attn_w, tn):
    i = pl.program_id(0)
    tm = x_ref.shape[0]
    conv_w = conv_ref.shape[1]

    @pl.when(i == 0)
    def _():
        z_ref[0:SUBLANES, :] = jnp.zeros((SUBLANES, conv_w), F32)

    h_ref[...] = _rms(x_ref[...], g_ref[...]).astype(BF16)
    hn = h_ref[...]

    def proj(col):
        return jnp.dot(hn, w_ref[:, col:col + tn], preferred_element_type=F32)

    def rope(t):
        up = pltpu.roll(t, LANES - ROT_DIM // 2, axis=1)
        dn = pltpu.roll(t, ROT_DIM // 2, axis=1)
        return t * c_ref[...] + up * s1_ref[...] + dn * s2_ref[...]

    ones = jnp.ones((VT_PAD, tm), BF16)
    for n in range(3 * attn_w // tn):
        acc = proj(n * tn)
        kind, base = divmod(n * tn, attn_w)
        for c in range(tn // HEAD_W):
            t = acc[:, c * HEAD_W:(c + 1) * HEAD_W]
            head = base // HEAD_W + c
            if kind == 0:
                qt_ref[head * HEAD_W:(head + 1) * HEAD_W, :] = (rope(t) * q_scale).T.astype(BF16)
            elif kind == 1:
                k_ref[:, head * HEAD_W:(head + 1) * HEAD_W] = rope(t).astype(BF16)
            else:
                r0 = head * VT_ROWS
                vt_ref[r0:r0 + HEAD_W, :] = t.T.astype(BF16)
                vt_ref[r0 + HEAD_W:r0 + VT_ROWS, :] = ones

    for n in range(conv_w // tn):
        cs = slice(n * tn, (n + 1) * tn)
        gate_b = proj(3 * attn_w + n * tn)
        z = proj(3 * attn_w + conv_w + n * tn) * proj(3 * attn_w + 2 * conv_w + n * tn)
        z_ref[SUBLANES:SUBLANES + tm, cs] = z
        z1 = z_ref[SUBLANES - 1:SUBLANES - 1 + tm, cs]
        z2 = z_ref[SUBLANES - 2:SUBLANES - 2 + tm, cs]
        cw = cw_ref[:, cs]
        y = cw[0:1, :] * z2 + cw[1:2, :] * z1 + cw[2:3, :] * z
        conv_ref[:, cs] = (gate_b * y).astype(BF16)
        z_ref[0:SUBLANES, cs] = z_ref[tm:tm + SUBLANES, cs]


def _inproj(x, g, w_in, layer, rope_c, rope_s1, rope_s2, conv_w, attn_w, *, tm, tn=INPROJ_COLS):
    s, d = x.shape
    conv_width = conv_w.shape[1]
    tm = min(tm, s)
    tn = min(tn, attn_w, conv_width)
    assert s % tm == 0 and attn_w % tn == 0 and conv_width % tn == 0
    assert tn % HEAD_W == 0 and HEAD_W == LANES
    assert w_in.shape[2] == 3 * attn_w + 3 * conv_width
    q_scale = DIFF_HEAD_DIM ** -0.5 * LOG2E
    n_heads = attn_w // HEAD_W
    tab = pl.BlockSpec((tm, LANES), lambda i: (i, 0))
    return pl.pallas_call(
        functools.partial(_inproj_kernel, q_scale=q_scale, attn_w=attn_w, tn=tn),
        grid=(s // tm,),
        in_specs=[
            pl.BlockSpec((tm, d), lambda i: (i, 0)),
            pl.BlockSpec((1, d), lambda i: (0, 0)),
            pl.BlockSpec((None, d, w_in.shape[2]), lambda i: (layer, 0, 0),
                         pipeline_mode=pl.Buffered(1)),
            tab, tab, tab,
            pl.BlockSpec((CONV_K, conv_width), lambda i: (0, 0)),
        ],
        out_specs=[
            pl.BlockSpec((attn_w, tm), lambda i: (0, i)),
            pl.BlockSpec((tm, attn_w), lambda i: (i, 0)),
            pl.BlockSpec((None, n_heads * VT_ROWS, tm), lambda i: (i, 0, 0)),
            pl.BlockSpec((tm, conv_width), lambda i: (i, 0)),
        ],
        out_shape=[
            jax.ShapeDtypeStruct((attn_w, s), BF16),
            jax.ShapeDtypeStruct((s, attn_w), BF16),
            jax.ShapeDtypeStruct((s // tm, n_heads * VT_ROWS, tm), BF16),
            jax.ShapeDtypeStruct((s, conv_width), BF16),
        ],
        scratch_shapes=[
            pltpu.VMEM((tm, d), BF16),
            pltpu.VMEM((tm + SUBLANES, conv_width), F32),
        ],
        compiler_params=_params("arbitrary"),
        name="in_proj",
    )(x, g.reshape(1, d), w_in, rope_c, rope_s1, rope_s2, conv_w)


def _attn_kernel(*refs, ks, cw, depth, cast_blocks):
    n_cast = len(cast_blocks)
    (li_ref, qt_ref, k_ref, vt_ref, lq1_ref, lk1_ref, lq2_ref, lk2_ref, w_ref) = refs[:9]
    cast_in = refs[9:9 + n_cast]
    o_ref = refs[9 + n_cast]
    cast_out = refs[10 + n_cast:10 + 2 * n_cast]
    qs_ref, m_ref, top_ref, acc_ref, st_ref = refs[10 + 2 * n_cast:]
    i = pl.program_id(1)

    for src, dst, blk in zip(cast_in, cast_out, cast_blocks):
        if blk is None:
            dst[...] = src[...].astype(BF16)
        else:
            for f in range(src.shape[1] // blk):
                dst[f] = src[:, f * blk:(f + 1) * blk].astype(BF16)

    tq = qt_ref.shape[1]
    tk = vt_ref.shape[2]

    qt = qt_ref[...]
    row = lax.broadcasted_iota(jnp.int32, qt.shape, 0)
    zero = jnp.zeros_like(qt)
    qs_ref[:, 0:tq] = jnp.where(row < DIFF_HEAD_DIM, qt, zero)
    qs_ref[:, tq:2 * tq] = jnp.where(row >= DIFF_HEAD_DIM, qt, zero)

    nq = tq // tk
    all_units = [(t, c) for t in range(tk // ks) for c in range(2 * tq // cw)]

    def col_pos(c):
        q0 = (c * cw) % tq
        return q0 // tk, q0 % tk

    def diag_units(d):
        keep = []
        for (t, c) in all_units:
            qb, co = col_pos(c)
            if qb > d or (qb == d and t * ks < co + cw):
                keep.append((t, c))
        return keep

    def scores(j, t, c):
        k = k_ref[pl.ds(pl.multiple_of(j * tk + t * ks, ks), ks), :]
        return jnp.dot(k, qs_ref[:, c * cw:(c + 1) * cw], preferred_element_type=F32)

    def masked(t, c, st, diag):
        qb, co = col_pos(c)
        if diag is not None and qb == diag and (t + 1) * ks - 1 > co:
            key = lax.broadcasted_iota(jnp.int32, (ks, cw), 0) + t * ks
            qry = lax.broadcasted_iota(jnp.int32, (ks, cw), 1) + co
            st = jnp.where(key <= qry, st, NEG_INF)
        return st

    def pv(j, t, pt):
        vt = vt_ref[j, :, t * ks:(t + 1) * ks]
        return jnp.dot(vt, pt.astype(BF16), preferred_element_type=F32)

    def unshifted_unit(j, t, c, st, diag):
        cs = slice(c * cw, (c + 1) * cw)
        st = masked(t, c, st, diag)
        top_ref[:, cs] = jnp.maximum(top_ref[:, cs], jnp.max(st, axis=0, keepdims=True))
        acc_ref[:, cs] += pv(j, t, jnp.exp2(st))

    def running_max_unit(j, t, c, st, diag):
        cs = slice(c * cw, (c + 1) * cw)
        st = masked(t, c, st, diag)
        m_prev = m_ref[:, cs]
        m_new = jnp.maximum(m_prev, jnp.max(st, axis=0, keepdims=True))
        alpha = jnp.exp2(m_prev - m_new)
        acc_ref[:, cs] = alpha * acc_ref[:, cs] + pv(j, t, jnp.exp2(st - m_new))
        m_ref[:, cs] = m_new

    def at_slab(dj, units):
        return [(dj, t, c) for (t, c) in units]

    def attend(unit_fn):
        acc_ref[...] = jnp.zeros(acc_ref.shape, F32)

        def step(j, units, diag, next_units):
            pending = [st_ref[d] for d in range(depth)]
            for n, (dj, t, c) in enumerate(units):
                st = pending.pop(0)
                ahead = n + depth
                if ahead < len(units):
                    aj, at, ac = units[ahead]
                    pending.append(scores(j + aj, at, ac))
                elif next_units is not None:
                    aj, at, ac = next_units[ahead - len(units)]
                    st_ref[ahead - len(units)] = scores(j + aj, at, ac)
                unit_fn(j + dj, t, c, st, diag)

        per = 2 if nq % 2 == 0 else 1
        loop_units = [u for dj in range(per) for u in at_slab(dj, all_units)]

        def body(jj, carry):
            step(jj * per, loop_units, None, at_slab(per, all_units))
            return carry

        for d in range(depth):
            st_ref[d] = scores(0, *all_units[d])
        lax.fori_loop(0, i * nq // per, body, 0)
        for d in range(nq):
            nxt = at_slab(1, diag_units(d + 1)) if d + 1 < nq else None
            step(i * nq + d, at_slab(0, diag_units(d)), d, nxt)

        lam = (jnp.exp(jnp.sum(lq1_ref[...] * lk1_ref[...], axis=-1, keepdims=True))
               - jnp.exp(jnp.sum(lq2_ref[...] * lk2_ref[...], axis=-1, keepdims=True))
               + li_ref[0])
        inv_l = 1.0 / acc_ref[HEAD_W:HEAD_W + 1, :]
        o1 = acc_ref[0:HEAD_W, 0:tq] * inv_l[:, 0:tq]
        o2 = acc_ref[0:HEAD_W, tq:2 * tq] * inv_l[:, tq:2 * tq]
        ot = o1 - lam * o2
        ms = jnp.mean(ot * ot, axis=0, keepdims=True)
        ot = ot * lax.rsqrt(ms + EPS) * w_ref[...] * (1.0 - li_ref[0])
        o_ref[...] = ot.T.astype(BF16)

    top_ref[...] = jnp.full(top_ref.shape, NEG_INF, F32)
    attend(unshifted_unit)

    top = top_ref[...]
    in_range = jnp.logical_and(jnp.max(top) <= EXP2_HEADROOM, jnp.min(top) >= -EXP2_HEADROOM)

    @pl.when(jnp.logical_not(in_range))
    def _():
        m_ref[...] = jnp.full(m_ref.shape, NEG_INF, F32)
        attend(running_max_unit)


def _attn_tq(s, tk, tq=ATTN_Q):
    return max(min(tq, s), tk)


def _castable(w, n_steps):
    rows = w.shape[1]
    return rows % n_steps == 0 and (rows // n_steps) % 16 == 0


def _attention(qt, k, vt, lam_init, lq1, lk1, lq2, lk2, subln_w, n_heads, cast=(), *,
               ks=ATTN_UNIT_KEYS, cw=ATTN_UNIT_COLS, depth=ATTN_DEPTH):
    s = k.shape[0]
    tk = vt.shape[2]
    tq, ks, cw = _attn_tq(s, tk), min(ks, tk), min(cw, tk)
    assert s % tq == 0 and tq % tk == 0 and tk % ks == 0
    dh = DIFF_HEAD_DIM
    ni = s // tq
    n_steps = n_heads * ni
    lspec = pl.BlockSpec((1, dh), lambda h, i: (0, 0))
    cast_in_specs, cast_out_specs, cast_out_shapes = [], [], []
    for w, layer, blk in cast:
        rows, cols = w.shape[1] // n_steps, w.shape[2]
        cast_in_specs.append(
            pl.BlockSpec((None, rows, cols), lambda h, i, layer=layer: (layer, h * ni + i, 0)))
        if blk is None:
            cast_out_specs.append(pl.BlockSpec((rows, cols), lambda h, i: (h * ni + i, 0)))
            cast_out_shapes.append(jax.ShapeDtypeStruct(w.shape[1:], BF16))
        else:
            cast_out_specs.append(
                pl.BlockSpec((cols // blk, rows, blk), lambda h, i: (0, h * ni + i, 0)))
            cast_out_shapes.append(jax.ShapeDtypeStruct((cols // blk, w.shape[1], blk), BF16))
    outs = pl.pallas_call(
        functools.partial(_attn_kernel, ks=ks, cw=cw, depth=depth,
                          cast_blocks=tuple(blk for _, _, blk in cast)),
        grid=(n_heads, ni),
        in_specs=[
            pl.BlockSpec(memory_space=pltpu.SMEM),
            pl.BlockSpec((HEAD_W, tq), lambda h, i: (h, i)),
            pl.BlockSpec((s, HEAD_W), lambda h, i: (0, h)),
            pl.BlockSpec((s // tk, VT_ROWS, tk), lambda h, i: (0, h, 0)),
            lspec, lspec, lspec, lspec,
            pl.BlockSpec((HEAD_W, 1), lambda h, i: (0, 0)),
        ] + cast_in_specs,
        out_specs=[pl.BlockSpec((tq, HEAD_W), lambda h, i: (i, h))] + cast_out_specs,
        out_shape=[jax.ShapeDtypeStruct((s, n_heads * HEAD_W), BF16)] + cast_out_shapes,
        scratch_shapes=[
            pltpu.VMEM((HEAD_W, 2 * tq), BF16),
            pltpu.VMEM((1, 2 * tq), F32),
            pltpu.VMEM((1, 2 * tq), F32),
            pltpu.VMEM((VT_ROWS, 2 * tq), F32),
            pltpu.VMEM((depth, ks, cw), F32),
        ],
        compiler_params=_params("arbitrary", "arbitrary"),
        name="diff_attn",
    )(lam_init, qt, k, vt, lq1.reshape(1, dh), lk1.reshape(1, dh),
      lq2.reshape(1, dh), lk2.reshape(1, dh), subln_w.reshape(HEAD_W, 1),
      *[w for w, _, _ in cast])
    return outs[0], outs[1:]


def _out_kernel(x_ref, a_ref, c_ref, wa_ref, wc_ref, o_ref):
    o_ref[...] = (x_ref[...]
                  + jnp.dot(a_ref[...], wa_ref[...], preferred_element_type=F32)
                  + jnp.dot(c_ref[...], wc_ref[...], preferred_element_type=F32))


def _out_proj(x, attn, conv, w_out, layer, *, tm=OUTPROJ_ROWS):
    s, d = x.shape
    aw, cw = attn.shape[1], conv.shape[1]
    tm = min(tm, s)
    assert s % tm == 0 and aw == cw
    return pl.pallas_call(
        _out_kernel,
        grid=(s // tm,),
        in_specs=[
            pl.BlockSpec((tm, d), lambda i: (i, 0)),
            pl.BlockSpec((tm, aw), lambda i: (i, 0)),
            pl.BlockSpec((tm, cw), lambda i: (i, 0)),
            pl.BlockSpec((None, aw, d), lambda i: (layer, 0, 0)),
            pl.BlockSpec((None, cw, d), lambda i: (layer, 1, 0)),
        ],
        out_specs=pl.BlockSpec((tm, d), lambda i: (i, 0)),
        out_shape=jax.ShapeDtypeStruct((s, d), F32),
        compiler_params=_params("parallel"),
        name="out_proj",
    )(x, attn, conv, w_out, w_out)


def _rope_lane_tables(seq):
    half = ROT_DIM // 2
    pos = jnp.arange(seq, dtype=F32)
    inv_freq = ROPE_THETA ** (-jnp.arange(0, ROT_DIM, 2, dtype=F32) / ROT_DIM)
    ang = pos[:, None] * inv_freq[None, :]
    cos, sin = jnp.cos(ang), jnp.sin(ang)
    rest = DIFF_HEAD_DIM - ROT_DIM
    ones = jnp.ones((seq, rest), F32)
    c = jnp.concatenate([cos, cos, ones], axis=-1)
    s_up = jnp.concatenate([-sin, jnp.zeros((seq, half + rest), F32)], axis=-1)
    s_dn = jnp.concatenate([jnp.zeros((seq, half), F32), sin, jnp.zeros((seq, rest), F32)], axis=-1)
    rep = LANES // DIFF_HEAD_DIM
    return jnp.tile(c, (1, rep)), jnp.tile(s_up, (1, rep)), jnp.tile(s_dn, (1, rep))


def kernel(x, norm_ffn1, ffn1_w_gate, ffn1_w_up, ffn1_w_down, norm_mix, w_in, conv_w,
           lambda_q1, lambda_k1, lambda_q2, lambda_k2, subln_w, w_out,
           norm_ffn2, ffn2_w_gate, ffn2_w_up, ffn2_w_down, norm_final):
    b, s, d = x.shape
    depth = norm_ffn1.shape[0]
    attn_w = d // 2
    n_heads = attn_w // HEAD_W
    rope_c, rope_s1, rope_s2 = _rope_lane_tables(s)
    tm_in = min(INPROJ_ROWS, s)
    n_steps = n_heads * (s // _attn_tq(s, tm_in))
    tf = min(FFN_COLS, ffn1_w_gate.shape[2])
    ffn1 = ((ffn1_w_gate, tf), (ffn1_w_up, tf), (ffn1_w_down, None))
    ffn2 = ((ffn2_w_gate, tf), (ffn2_w_up, tf), (ffn2_w_down, None))

    def bf_layer(w, l, blk):
        wb = w[l].astype(BF16)
        if blk is None:
            return wb
        rows, cols = wb.shape
        return wb.reshape(rows, cols // blk, blk).transpose(1, 0, 2)

    outs = []
    for bi in range(b):
        xb = x[bi]
        cur = {"ffn1": [bf_layer(w, 0, blk) for w, blk in ffn1], "w_in": bf_layer(w_in, 0, None),
               "w_out": bf_layer(w_out, 0, None)}
        for l in range(depth):
            lam_init = jnp.full((1,), 0.8 - 0.6 * math.exp(-0.3 * l), F32)
            xb = _ffn(xb, norm_ffn1[l], *cur["ffn1"])
            qt, k, vt, conv = _inproj(xb, norm_mix[l], cur["w_in"][None], 0, rope_c, rope_s1,
                                      rope_s2, conv_w[l], attn_w, tm=tm_in)
            wanted = [(w, l, blk) for w, blk in ffn2]
            if l + 1 < depth:
                wanted += [(w, l + 1, blk) for w, blk in ffn1]
                wanted += [(w_in, l + 1, None), (w_out, l + 1, None)]
            in_kernel = [e for e in wanted if _castable(e[0], n_steps)]
            attn, casted = _attention(qt, k, vt, lam_init, lambda_q1[l], lambda_k1[l],
                                      lambda_q2[l], lambda_k2[l], subln_w[l], n_heads, in_kernel)
            casted = list(casted)
            ready = [casted.pop(0) if _castable(w, n_steps) else bf_layer(w, ll, blk)
                     for w, ll, blk in wanted]
            xb = _out_proj(xb, attn, conv, cur["w_out"][None], 0)
            g_final = norm_final if l == depth - 1 else None
            xb = _ffn(xb, norm_ffn2[l], *ready[:3], g_final)
            if l + 1 < depth:
                cur = {"ffn1": ready[3:6], "w_in": ready[6], "w_out": ready[7]}
        outs.append(xb)
    return outs[0][None] if b == 1 else jnp.stack(outs, axis=0)
```

```python
import functools
import math

import jax
import jax.numpy as jnp
from jax import lax
from jax.experimental import pallas as pl
from jax.experimental.pallas import tpu as pltpu

F32 = jnp.float32
BF16 = jnp.bfloat16

EPS = 1e-5
NEG_INF = -1e30
ROPE_THETA = 500000.0
DIFF_HEAD_DIM = 64
HEAD_W = 2 * DIFF_HEAD_DIM
ROT_DIM = DIFF_HEAD_DIM // 4
CONV_K = 3
LOG2E = math.log2(math.e)

EXP2_HEADROOM = 60.0
VT_PAD = 16
VT_ROWS = HEAD_W + VT_PAD
LANES = 128
SUBLANES = 8
VMEM_LIMIT = 56 * 1024 * 1024

FFN_ROWS, FFN_COLS = 1024, 512
NORM_ROWS = 256
INPROJ_ROWS, INPROJ_COLS = 512, 512
OUTPROJ_ROWS = 512
ATTN_Q = 2048
ATTN_UNIT_KEYS, ATTN_UNIT_COLS = 256, 256
ATTN_DEPTH = 4


def _params(*sem):
    return pltpu.CompilerParams(dimension_semantics=sem, vmem_limit_bytes=VMEM_LIMIT)


def _rms(x, g):
    ms = jnp.mean(x * x, axis=-1, keepdims=True)
    return x * lax.rsqrt(ms + EPS) * g


def _ffn_kernel(*refs, final_norm, rows, nf):
    if final_norm:
        x_ref, g_ref, wg_ref, wu_ref, wd_ref, gf_ref, o_ref, h_ref = refs
    else:
        x_ref, g_ref, wg_ref, wu_ref, wd_ref, o_ref, h_ref = refs
    f = pl.program_id(1)
    tm = x_ref.shape[0]

    def block(first, last):
        step = rows if (first or last) else tm
        for r in range(0, tm, step):
            rs = slice(r, r + step)
            if first:
                h = _rms(x_ref[rs, :], g_ref[...]).astype(BF16)
                h_ref[rs, :] = h
            else:
                h = h_ref[rs, :]
            gate = jnp.dot(h, wg_ref[...], preferred_element_type=F32)
            up = jnp.dot(h, wu_ref[...], preferred_element_type=F32)
            act = (gate * (1.0 / (1.0 + jnp.exp(-gate))) * up).astype(BF16)
            o = jnp.dot(act, wd_ref[...], preferred_element_type=F32)
            if not first:
                o = o_ref[rs, :] + o
            if last:
                o = x_ref[rs, :] + 0.5 * o
                if final_norm:
                    o = _rms(o, gf_ref[...])
            o_ref[rs, :] = o

    pl.when(f == 0)(lambda: block(True, nf == 1))
    if nf > 2:
        pl.when(jnp.logical_and(f > 0, f < nf - 1))(lambda: block(False, False))
    if nf > 1:
        pl.when(f == nf - 1)(lambda: block(False, True))


def _ffn(x, g, wg, wu, wd, layer, g_final=None, *, tm=FFN_ROWS, tf=FFN_COLS):
    s, d = x.shape
    dff = wg.shape[2]
    tm, tf = min(tm, s), min(tf, dff)
    assert s % tm == 0 and dff % tf == 0
    final_norm = g_final is not None
    in_specs = [
        pl.BlockSpec((tm, d), lambda i, f: (i, 0)),
        pl.BlockSpec((1, d), lambda i, f: (0, 0)),
        pl.BlockSpec((None, d, tf), lambda i, f: (layer, 0, f)),
        pl.BlockSpec((None, d, tf), lambda i, f: (layer, 0, f)),
        pl.BlockSpec((None, tf, d), lambda i, f: (layer, f, 0)),
    ]
    args = [x, g.reshape(1, d), wg, wu, wd]
    if final_norm:
        in_specs.append(pl.BlockSpec((1, d), lambda i, f: (0, 0)))
        args.append(g_final.reshape(1, d))
    return pl.pallas_call(
        functools.partial(_ffn_kernel, final_norm=final_norm, rows=min(NORM_ROWS, tm),
                          nf=dff // tf),
        grid=(s // tm, dff // tf),
        in_specs=in_specs,
        out_specs=pl.BlockSpec((tm, d), lambda i, f: (i, 0)),
        out_shape=jax.ShapeDtypeStruct((s, d), F32),
        scratch_shapes=[pltpu.VMEM((tm, d), BF16)],
        compiler_params=_params("parallel", "arbitrary"),
        name="ffn_final" if final_norm else "ffn",
    )(*args)


def _inproj_kernel(x_ref, g_ref, w_ref, c_ref, s1_ref, s2_ref, cw_ref,
                   qt_ref, k_ref, vt_ref, conv_ref, h_ref, z_ref, *, q_scale, attn_w, tn):
    i = pl.program_id(0)
    tm = x_ref.shape[0]
    conv_w = conv_ref.shape[1]

    @pl.when(i == 0)
    def _():
        z_ref[0:SUBLANES, :] = jnp.zeros((SUBLANES, conv_w), F32)

    h_ref[...] = _rms(x_ref[...], g_ref[...]).astype(BF16)
    hn = h_ref[...]

    def proj(col):
        return jnp.dot(hn, w_ref[:, col:col + tn], preferred_element_type=F32)

    def rope(t):
        up = pltpu.roll(t, LANES - ROT_DIM // 2, axis=1)
        dn = pltpu.roll(t, ROT_DIM // 2, axis=1)
        return t * c_ref[...] + up * s1_ref[...] + dn * s2_ref[...]

    ones = jnp.ones((VT_PAD, tm), BF16)
    for n in range(3 * attn_w // tn):
        acc = proj(n * tn)
        kind, base = divmod(n * tn, attn_w)
        for c in range(tn // HEAD_W):
            t = acc[:, c * HEAD_W:(c + 1) * HEAD_W]
            head = base // HEAD_W + c
            if kind == 0:
                qt_ref[head * HEAD_W:(head + 1) * HEAD_W, :] = (rope(t) * q_scale).T.astype(BF16)
            elif kind == 1:
                k_ref[:, head * HEAD_W:(head + 1) * HEAD_W] = rope(t).astype(BF16)
            else:
                r0 = head * VT_ROWS
                vt_ref[r0:r0 + HEAD_W, :] = t.T.astype(BF16)
                vt_ref[r0 + HEAD_W:r0 + VT_ROWS, :] = ones

    for n in range(conv_w // tn):
        cs = slice(n * tn, (n + 1) * tn)
        gate_b = proj(3 * attn_w + n * tn)
        z = proj(3 * attn_w + conv_w + n * tn) * proj(3 * attn_w + 2 * conv_w + n * tn)
        z_ref[SUBLANES:SUBLANES + tm, cs] = z
        z1 = z_ref[SUBLANES - 1:SUBLANES - 1 + tm, cs]
        z2 = z_ref[SUBLANES - 2:SUBLANES - 2 + tm, cs]
        cw = cw_ref[:, cs]
        y = cw[0:1, :] * z2 + cw[1:2, :] * z1 + cw[2:3, :] * z
        conv_ref[:, cs] = (gate_b * y).astype(BF16)
        z_ref[0:SUBLANES, cs] = z_ref[tm:tm + SUBLANES, cs]


def _inproj(x, g, w_in, layer, rope_c, rope_s1, rope_s2, conv_w, attn_w, *, tm, tn=INPROJ_COLS):
    s, d = x.shape
    conv_width = conv_w.shape[1]
    tm = min(tm, s)
    tn = min(tn, attn_w, conv_width)
    assert s % tm == 0 and attn_w % tn == 0 and conv_width % tn == 0
    assert tn % HEAD_W == 0 and HEAD_W == LANES
    assert w_in.shape[2] == 3 * attn_w + 3 * conv_width
    q_scale = DIFF_HEAD_DIM ** -0.5 * LOG2E
    n_heads = attn_w // HEAD_W
    tab = pl.BlockSpec((tm, LANES), lambda i: (i, 0))
    return pl.pallas_call(
        functools.partial(_inproj_kernel, q_scale=q_scale, attn_w=attn_w, tn=tn),
        grid=(s // tm,),
        in_specs=[
            pl.BlockSpec((tm, d), lambda i: (i, 0)),
            pl.BlockSpec((1, d), lambda i: (0, 0)),
            pl.BlockSpec((None, d, w_in.shape[2]), lambda i: (layer, 0, 0),
                         pipeline_mode=pl.Buffered(1)),
            tab, tab, tab,
            pl.BlockSpec((CONV_K, conv_width), lambda i: (0, 0)),
        ],
        out_specs=[
            pl.BlockSpec((attn_w, tm), lambda i: (0, i)),
            pl.BlockSpec((tm, attn_w), lambda i: (i, 0)),
            pl.BlockSpec((None, n_heads * VT_ROWS, tm), lambda i: (i, 0, 0)),
            pl.BlockSpec((tm, conv_width), lambda i: (i, 0)),
        ],
        out_shape=[
            jax.ShapeDtypeStruct((attn_w, s), BF16),
            jax.ShapeDtypeStruct((s, attn_w), BF16),
            jax.ShapeDtypeStruct((s // tm, n_heads * VT_ROWS, tm), BF16),
            jax.ShapeDtypeStruct((s, conv_width), BF16),
        ],
        scratch_shapes=[
            pltpu.VMEM((tm, d), BF16),
            pltpu.VMEM((tm + SUBLANES, conv_width), F32),
        ],
        compiler_params=_params("arbitrary"),
        name="in_proj",
    )(x, g.reshape(1, d), w_in, rope_c, rope_s1, rope_s2, conv_w)


def _attn_kernel(*refs, ks, cw, depth, n_cast):
    (li_ref, qt_ref, k_ref, vt_ref, lq1_ref, lk1_ref, lq2_ref, lk2_ref, w_ref) = refs[:9]
    cast_in = refs[9:9 + n_cast]
    o_ref = refs[9 + n_cast]
    cast_out = refs[10 + n_cast:10 + 2 * n_cast]
    qs_ref, m_ref, top_ref, acc_ref, st_ref = refs[10 + 2 * n_cast:]
    i = pl.program_id(1)

    for src, dst in zip(cast_in, cast_out):
        dst[...] = src[...].astype(BF16)

    tq = qt_ref.shape[1]
    tk = vt_ref.shape[2]

    qt = qt_ref[...]
    row = lax.broadcasted_iota(jnp.int32, qt.shape, 0)
    zero = jnp.zeros_like(qt)
    qs_ref[:, 0:tq] = jnp.where(row < DIFF_HEAD_DIM, qt, zero)
    qs_ref[:, tq:2 * tq] = jnp.where(row >= DIFF_HEAD_DIM, qt, zero)

    nq = tq // tk
    all_units = [(t, c) for t in range(tk // ks) for c in range(2 * tq // cw)]

    def col_pos(c):
        q0 = (c * cw) % tq
        return q0 // tk, q0 % tk

    def diag_units(d):
        keep = []
        for (t, c) in all_units:
            qb, co = col_pos(c)
            if qb > d or (qb == d and t * ks < co + cw):
                keep.append((t, c))
        return keep

    def scores(j, t, c):
        k = k_ref[pl.ds(pl.multiple_of(j * tk + t * ks, ks), ks), :]
        return jnp.dot(k, qs_ref[:, c * cw:(c + 1) * cw], preferred_element_type=F32)

    def masked(t, c, st, diag):
        qb, co = col_pos(c)
        if diag is not None and qb == diag and (t + 1) * ks - 1 > co:
            key = lax.broadcasted_iota(jnp.int32, (ks, cw), 0) + t * ks
            qry = lax.broadcasted_iota(jnp.int32, (ks, cw), 1) + co
            st = jnp.where(key <= qry, st, NEG_INF)
        return st

    def pv(j, t, pt):
        vt = vt_ref[j, :, t * ks:(t + 1) * ks]
        return jnp.dot(vt, pt.astype(BF16), preferred_element_type=F32)

    def unshifted_unit(j, t, c, st, diag):
        cs = slice(c * cw, (c + 1) * cw)
        st = masked(t, c, st, diag)
        top_ref[:, cs] = jnp.maximum(top_ref[:, cs], jnp.max(st, axis=0, keepdims=True))
        acc_ref[:, cs] += pv(j, t, jnp.exp2(st))

    def running_max_unit(j, t, c, st, diag):
        cs = slice(c * cw, (c + 1) * cw)
        st = masked(t, c, st, diag)
        m_prev = m_ref[:, cs]
        m_new = jnp.maximum(m_prev, jnp.max(st, axis=0, keepdims=True))
        alpha = jnp.exp2(m_prev - m_new)
        acc_ref[:, cs] = alpha * acc_ref[:, cs] + pv(j, t, jnp.exp2(st - m_new))
        m_ref[:, cs] = m_new

    def at_slab(dj, units):
        return [(dj, t, c) for (t, c) in units]

    def attend(unit_fn):
        acc_ref[...] = jnp.zeros(acc_ref.shape, F32)

        def step(j, units, diag, next_units):
            pending = [st_ref[d] for d in range(depth)]
            for n, (dj, t, c) in enumerate(units):
                st = pending.pop(0)
                ahead = n + depth
                if ahead < len(units):
                    aj, at, ac = units[ahead]
                    pending.append(scores(j + aj, at, ac))
                elif next_units is not None:
                    aj, at, ac = next_units[ahead - len(units)]
                    st_ref[ahead - len(units)] = scores(j + aj, at, ac)
                unit_fn(j + dj, t, c, st, diag)

        per = 2 if nq % 2 == 0 else 1
        loop_units = [u for dj in range(per) for u in at_slab(dj, all_units)]

        def body(jj, carry):
            step(jj * per, loop_units, None, at_slab(per, all_units))
            return carry

        for d in range(depth):
            st_ref[d] = scores(0, *all_units[d])
        lax.fori_loop(0, i * nq // per, body, 0)
        for d in range(nq):
            nxt = at_slab(1, diag_units(d + 1)) if d + 1 < nq else None
            step(i * nq + d, at_slab(0, diag_units(d)), d, nxt)

        lam = (jnp.exp(jnp.sum(lq1_ref[...] * lk1_ref[...], axis=-1, keepdims=True))
               - jnp.exp(jnp.sum(lq2_ref[...] * lk2_ref[...], axis=-1, keepdims=True))
               + li_ref[0])
        inv_l = 1.0 / acc_ref[HEAD_W:HEAD_W + 1, :]
        o1 = acc_ref[0:HEAD_W, 0:tq] * inv_l[:, 0:tq]
        o2 = acc_ref[0:HEAD_W, tq:2 * tq] * inv_l[:, tq:2 * tq]
        ot = o1 - lam * o2
        ms = jnp.mean(ot * ot, axis=0, keepdims=True)
        ot = ot * lax.rsqrt(ms + EPS) * w_ref[...] * (1.0 - li_ref[0])
        o_ref[...] = ot.T.astype(BF16)

    top_ref[...] = jnp.full(top_ref.shape, NEG_INF, F32)
    attend(unshifted_unit)

    top = top_ref[...]
    in_range = jnp.logical_and(jnp.max(top) <= EXP2_HEADROOM, jnp.min(top) >= -EXP2_HEADROOM)

    @pl.when(jnp.logical_not(in_range))
    def _():
        m_ref[...] = jnp.full(m_ref.shape, NEG_INF, F32)
        attend(running_max_unit)


def _attn_tq(s, tk, tq=ATTN_Q):
    return max(min(tq, s), tk)


def _castable(w, n_steps):
    rows = w.shape[1]
    return rows % n_steps == 0 and (rows // n_steps) % 16 == 0


def _attention(qt, k, vt, lam_init, lq1, lk1, lq2, lk2, subln_w, n_heads, cast=(), *,
               ks=ATTN_UNIT_KEYS, cw=ATTN_UNIT_COLS, depth=ATTN_DEPTH):
    s = k.shape[0]
    tk = vt.shape[2]
    tq, ks, cw = _attn_tq(s, tk), min(ks, tk), min(cw, tk)
    assert s % tq == 0 and tq % tk == 0 and tk % ks == 0
    dh = DIFF_HEAD_DIM
    ni = s // tq
    n_steps = n_heads * ni
    lspec = pl.BlockSpec((1, dh), lambda h, i: (0, 0))
    cast_in_specs, cast_out_specs, cast_out_shapes = [], [], []
    for w, layer in cast:
        rows, cols = w.shape[1] // n_steps, w.shape[2]
        cast_in_specs.append(
            pl.BlockSpec((None, rows, cols), lambda h, i, layer=layer: (layer, h * ni + i, 0)))
        cast_out_specs.append(pl.BlockSpec((rows, cols), lambda h, i: (h * ni + i, 0)))
        cast_out_shapes.append(jax.ShapeDtypeStruct(w.shape[1:], BF16))
    outs = pl.pallas_call(
        functools.partial(_attn_kernel, ks=ks, cw=cw, depth=depth, n_cast=len(cast)),
        grid=(n_heads, ni),
        in_specs=[
            pl.BlockSpec(memory_space=pltpu.SMEM),
            pl.BlockSpec((HEAD_W, tq), lambda h, i: (h, i)),
            pl.BlockSpec((s, HEAD_W), lambda h, i: (0, h)),
            pl.BlockSpec((s // tk, VT_ROWS, tk), lambda h, i: (0, h, 0)),
            lspec, lspec, lspec, lspec,
            pl.BlockSpec((HEAD_W, 1), lambda h, i: (0, 0)),
        ] + cast_in_specs,
        out_specs=[pl.BlockSpec((tq, HEAD_W), lambda h, i: (i, h))] + cast_out_specs,
        out_shape=[jax.ShapeDtypeStruct((s, n_heads * HEAD_W), BF16)] + cast_out_shapes,
        scratch_shapes=[
            pltpu.VMEM((HEAD_W, 2 * tq), BF16),
            pltpu.VMEM((1, 2 * tq), F32),
            pltpu.VMEM((1, 2 * tq), F32),
            pltpu.VMEM((VT_ROWS, 2 * tq), F32),
            pltpu.VMEM((depth, ks, cw), F32),
        ],
        compiler_params=_params("arbitrary", "arbitrary"),
        name="diff_attn",
    )(lam_init, qt, k, vt, lq1.reshape(1, dh), lk1.reshape(1, dh),
      lq2.reshape(1, dh), lk2.reshape(1, dh), subln_w.reshape(HEAD_W, 1),
      *[w for w, _ in cast])
    return outs[0], outs[1:]


def _out_kernel(x_ref, a_ref, c_ref, wa_ref, wc_ref, o_ref):
    o_ref[...] = (x_ref[...]
                  + jnp.dot(a_ref[...], wa_ref[...], preferred_element_type=F32)
                  + jnp.dot(c_ref[...], wc_ref[...], preferred_element_type=F32))


def _out_proj(x, attn, conv, w_out, layer, *, tm=OUTPROJ_ROWS):
    s, d = x.shape
    aw, cw = attn.shape[1], conv.shape[1]
    tm = min(tm, s)
    assert s % tm == 0 and aw == cw
    return pl.pallas_call(
        _out_kernel,
        grid=(s // tm,),
        in_specs=[
            pl.BlockSpec((tm, d), lambda i: (i, 0)),
            pl.BlockSpec((tm, aw), lambda i: (i, 0)),
            pl.BlockSpec((tm, cw), lambda i: (i, 0)),
            pl.BlockSpec((None, aw, d), lambda i: (layer, 0, 0)),
            pl.BlockSpec((None, cw, d), lambda i: (layer, 1, 0)),
        ],
        out_specs=pl.BlockSpec((tm, d), lambda i: (i, 0)),
        out_shape=jax.ShapeDtypeStruct((s, d), F32),
        compiler_params=_params("parallel"),
        name="out_proj",
    )(x, attn, conv, w_out, w_out)


def _rope_lane_tables(seq):
    half = ROT_DIM // 2
    pos = jnp.arange(seq, dtype=F32)
    inv_freq = ROPE_THETA ** (-jnp.arange(0, ROT_DIM, 2, dtype=F32) / ROT_DIM)
    ang = pos[:, None] * inv_freq[None, :]
    cos, sin = jnp.cos(ang), jnp.sin(ang)
    rest = DIFF_HEAD_DIM - ROT_DIM
    ones = jnp.ones((seq, rest), F32)
    c = jnp.concatenate([cos, cos, ones], axis=-1)
    s_up = jnp.concatenate([-sin, jnp.zeros((seq, half + rest), F32)], axis=-1)
    s_dn = jnp.concatenate([jnp.zeros((seq, half), F32), sin, jnp.zeros((seq, rest), F32)], axis=-1)
    rep = LANES // DIFF_HEAD_DIM
    return jnp.tile(c, (1, rep)), jnp.tile(s_up, (1, rep)), jnp.tile(s_dn, (1, rep))


def kernel(x, norm_ffn1, ffn1_w_gate, ffn1_w_up, ffn1_w_down, norm_mix, w_in, conv_w,
           lambda_q1, lambda_k1, lambda_q2, lambda_k2, subln_w, w_out,
           norm_ffn2, ffn2_w_gate, ffn2_w_up, ffn2_w_down, norm_final):
    b, s, d = x.shape
    depth = norm_ffn1.shape[0]
    attn_w = d // 2
    n_heads = attn_w // HEAD_W
    rope_c, rope_s1, rope_s2 = _rope_lane_tables(s)
    tm_in = min(INPROJ_ROWS, s)
    n_steps = n_heads * (s // _attn_tq(s, tm_in))
    ffn1 = (ffn1_w_gate, ffn1_w_up, ffn1_w_down)
    ffn2 = (ffn2_w_gate, ffn2_w_up, ffn2_w_down)

    def bf_layer(w, l):
        return w[l].astype(BF16)[None]

    outs = []
    for bi in range(b):
        xb = x[bi]
        cur = {"ffn1": [bf_layer(w, 0) for w in ffn1], "w_in": bf_layer(w_in, 0),
               "w_out": bf_layer(w_out, 0)}
        for l in range(depth):
            lam_init = jnp.full((1,), 0.8 - 0.6 * math.exp(-0.3 * l), F32)
            xb = _ffn(xb, norm_ffn1[l], *cur["ffn1"], 0)
            qt, k, vt, conv = _inproj(xb, norm_mix[l], cur["w_in"], 0, rope_c, rope_s1, rope_s2,
                                      conv_w[l], attn_w, tm=tm_in)
            wanted = [(w, l) for w in ffn2]
            if l + 1 < depth:
                wanted += [(w, l + 1) for w in ffn1] + [(w_in, l + 1), (w_out, l + 1)]
            in_kernel = [(w, ll) for w, ll in wanted if _castable(w, n_steps)]
            attn, casted = _attention(qt, k, vt, lam_init, lambda_q1[l], lambda_k1[l],
                                      lambda_q2[l], lambda_k2[l], subln_w[l], n_heads, in_kernel)
            casted = list(casted)
            ready = [casted.pop(0)[None] if _castable(w, n_steps) else bf_layer(w, ll)
                     for w, ll in wanted]
            xb = _out_proj(xb, attn, conv, cur["w_out"], 0)
            g_final = norm_final if l == depth - 1 else None
            xb = _ffn(xb, norm_ffn2[l], *ready[:3], 0, g_final)
            if l + 1 < depth:
                cur = {"ffn1": ready[3:6], "w_in": ready[6], "w_out": ready[7]}
        outs.append(xb)
    return outs[0][None] if b == 1 else jnp.stack(outs, axis=0)
```

```python
import functools
import math

import jax
import jax.numpy as jnp
from jax import lax
from jax.experimental import pallas as pl
from jax.experimental.pallas import tpu as pltpu

F32 = jnp.float32
BF16 = jnp.bfloat16

EPS = 1e-5
NEG_INF = -1e30
ROPE_THETA = 500000.0
DIFF_HEAD_DIM = 64
HEAD_W = 2 * DIFF_HEAD_DIM
ROT_DIM = DIFF_HEAD_DIM // 4
CONV_K = 3
LOG2E = math.log2(math.e)

EXP2_HEADROOM = 60.0
VT_PAD = 16
VT_ROWS = HEAD_W + VT_PAD
LANES = 128
SUBLANES = 8
VMEM_LIMIT = 56 * 1024 * 1024

FFN_ROWS, FFN_COLS = 1024, 512
NORM_ROWS = 512
INPROJ_ROWS, INPROJ_COLS = 512, 512
OUTPROJ_ROWS = 512
ATTN_Q = 2048
ATTN_UNIT_KEYS, ATTN_UNIT_COLS = 256, 256
ATTN_DEPTH = 4


def _params(*sem):
    return pltpu.CompilerParams(dimension_semantics=sem, vmem_limit_bytes=VMEM_LIMIT)


def _rms(x, g):
    ms = jnp.mean(x * x, axis=-1, keepdims=True)
    return x * lax.rsqrt(ms + EPS) * g


def _ffn_kernel(*refs, final_norm, rows, nf):
    if final_norm:
        x_ref, g_ref, wg_ref, wu_ref, wd_ref, gf_ref, o_ref, h_ref = refs
    else:
        x_ref, g_ref, wg_ref, wu_ref, wd_ref, o_ref, h_ref = refs
    f = pl.program_id(1)
    tm = x_ref.shape[0]

    def block(first, last):
        step = rows if (first or last) else tm
        for r in range(0, tm, step):
            rs = slice(r, r + step)
            if first:
                h = _rms(x_ref[rs, :], g_ref[...]).astype(BF16)
                h_ref[rs, :] = h
            else:
                h = h_ref[rs, :]
            gate = jnp.dot(h, wg_ref[...], preferred_element_type=F32)
            up = jnp.dot(h, wu_ref[...], preferred_element_type=F32)
            act = (gate * (1.0 / (1.0 + jnp.exp(-gate))) * up).astype(BF16)
            o = jnp.dot(act, wd_ref[...], preferred_element_type=F32)
            if not first:
                o = o_ref[rs, :] + o
            if last:
                o = x_ref[rs, :] + 0.5 * o
                if final_norm:
                    o = _rms(o, gf_ref[...])
            o_ref[rs, :] = o

    pl.when(f == 0)(lambda: block(True, nf == 1))
    if nf > 2:
        pl.when(jnp.logical_and(f > 0, f < nf - 1))(lambda: block(False, False))
    if nf > 1:
        pl.when(f == nf - 1)(lambda: block(False, True))


def _ffn(x, g, wg, wu, wd, layer, g_final=None, *, tm=FFN_ROWS, tf=FFN_COLS):
    s, d = x.shape
    dff = wg.shape[2]
    tm, tf = min(tm, s), min(tf, dff)
    assert s % tm == 0 and dff % tf == 0
    final_norm = g_final is not None
    in_specs = [
        pl.BlockSpec((tm, d), lambda i, f: (i, 0)),
        pl.BlockSpec((1, d), lambda i, f: (0, 0)),
        pl.BlockSpec((None, d, tf), lambda i, f: (layer, 0, f)),
        pl.BlockSpec((None, d, tf), lambda i, f: (layer, 0, f)),
        pl.BlockSpec((None, tf, d), lambda i, f: (layer, f, 0)),
    ]
    args = [x, g.reshape(1, d), wg, wu, wd]
    if final_norm:
        in_specs.append(pl.BlockSpec((1, d), lambda i, f: (0, 0)))
        args.append(g_final.reshape(1, d))
    return pl.pallas_call(
        functools.partial(_ffn_kernel, final_norm=final_norm, rows=min(NORM_ROWS, tm),
                          nf=dff // tf),
        grid=(s // tm, dff // tf),
        in_specs=in_specs,
        out_specs=pl.BlockSpec((tm, d), lambda i, f: (i, 0)),
        out_shape=jax.ShapeDtypeStruct((s, d), F32),
        scratch_shapes=[pltpu.VMEM((tm, d), BF16)],
        compiler_params=_params("parallel", "arbitrary"),
        name="ffn_final" if final_norm else "ffn",
    )(*args)


def _inproj_kernel(x_ref, g_ref, w_ref, c_ref, s1_ref, s2_ref, cw_ref,
                   qt_ref, k_ref, vt_ref, conv_ref, h_ref, z_ref, *, q_scale, attn_w, tn):
    i = pl.program_id(0)
    tm = x_ref.shape[0]
    conv_w = conv_ref.shape[1]

    @pl.when(i == 0)
    def _():
        z_ref[0:SUBLANES, :] = jnp.zeros((SUBLANES, conv_w), F32)

    h_ref[...] = _rms(x_ref[...], g_ref[...]).astype(BF16)
    hn = h_ref[...]

    def proj(col):
        return jnp.dot(hn, w_ref[:, col:col + tn], preferred_element_type=F32)

    def rope(t):
        up = pltpu.roll(t, LANES - ROT_DIM // 2, axis=1)
        dn = pltpu.roll(t, ROT_DIM // 2, axis=1)
        return t * c_ref[...] + up * s1_ref[...] + dn * s2_ref[...]

    ones = jnp.ones((VT_PAD, tm), BF16)
    for n in range(3 * attn_w // tn):
        acc = proj(n * tn)
        kind, base = divmod(n * tn, attn_w)
        for c in range(tn // HEAD_W):
            t = acc[:, c * HEAD_W:(c + 1) * HEAD_W]
            head = base // HEAD_W + c
            if kind == 0:
                qt_ref[head * HEAD_W:(head + 1) * HEAD_W, :] = (rope(t) * q_scale).T.astype(BF16)
            elif kind == 1:
                k_ref[:, head * HEAD_W:(head + 1) * HEAD_W] = rope(t).astype(BF16)
            else:
                r0 = head * VT_ROWS
                vt_ref[r0:r0 + HEAD_W, :] = t.T.astype(BF16)
                vt_ref[r0 + HEAD_W:r0 + VT_ROWS, :] = ones

    for n in range(conv_w // tn):
        cs = slice(n * tn, (n + 1) * tn)
        gate_b = proj(3 * attn_w + n * tn)
        z = proj(3 * attn_w + conv_w + n * tn) * proj(3 * attn_w + 2 * conv_w + n * tn)
        z_ref[SUBLANES:SUBLANES + tm, cs] = z
        z1 = z_ref[SUBLANES - 1:SUBLANES - 1 + tm, cs]
        z2 = z_ref[SUBLANES - 2:SUBLANES - 2 + tm, cs]
        cw = cw_ref[:, cs]
        y = cw[0:1, :] * z2 + cw[1:2, :] * z1 + cw[2:3, :] * z
        conv_ref[:, cs] = (gate_b * y).astype(BF16)
        z_ref[0:SUBLANES, cs] = z_ref[tm:tm + SUBLANES, cs]


def _inproj(x, g, w_in, layer, rope_c, rope_s1, rope_s2, conv_w, attn_w, *, tm, tn=INPROJ_COLS):
    s, d = x.shape
    conv_width = conv_w.shape[1]
    tm = min(tm, s)
    tn = min(tn, attn_w, conv_width)
    assert s % tm == 0 and attn_w % tn == 0 and conv_width % tn == 0
    assert tn % HEAD_W == 0 and HEAD_W == LANES
    assert w_in.shape[2] == 3 * attn_w + 3 * conv_width
    q_scale = DIFF_HEAD_DIM ** -0.5 * LOG2E
    n_heads = attn_w // HEAD_W
    tab = pl.BlockSpec((tm, LANES), lambda i: (i, 0))
    return pl.pallas_call(
        functools.partial(_inproj_kernel, q_scale=q_scale, attn_w=attn_w, tn=tn),
        grid=(s // tm,),
        in_specs=[
            pl.BlockSpec((tm, d), lambda i: (i, 0)),
            pl.BlockSpec((1, d), lambda i: (0, 0)),
            pl.BlockSpec((None, d, w_in.shape[2]), lambda i: (layer, 0, 0),
                         pipeline_mode=pl.Buffered(1)),
            tab, tab, tab,
            pl.BlockSpec((CONV_K, conv_width), lambda i: (0, 0)),
        ],
        out_specs=[
            pl.BlockSpec((attn_w, tm), lambda i: (0, i)),
            pl.BlockSpec((tm, attn_w), lambda i: (i, 0)),
            pl.BlockSpec((None, n_heads * VT_ROWS, tm), lambda i: (i, 0, 0)),
            pl.BlockSpec((tm, conv_width), lambda i: (i, 0)),
        ],
        out_shape=[
            jax.ShapeDtypeStruct((attn_w, s), BF16),
            jax.ShapeDtypeStruct((s, attn_w), BF16),
            jax.ShapeDtypeStruct((s // tm, n_heads * VT_ROWS, tm), BF16),
            jax.ShapeDtypeStruct((s, conv_width), BF16),
        ],
        scratch_shapes=[
            pltpu.VMEM((tm, d), BF16),
            pltpu.VMEM((tm + SUBLANES, conv_width), F32),
        ],
        compiler_params=_params("arbitrary"),
        name="in_proj",
    )(x, g.reshape(1, d), w_in, rope_c, rope_s1, rope_s2, conv_w)


def _attn_kernel(*refs, ks, cw, depth, n_cast):
    (li_ref, qt_ref, k_ref, vt_ref, lq1_ref, lk1_ref, lq2_ref, lk2_ref, w_ref) = refs[:9]
    cast_in = refs[9:9 + n_cast]
    o_ref = refs[9 + n_cast]
    cast_out = refs[10 + n_cast:10 + 2 * n_cast]
    qs_ref, m_ref, top_ref, acc_ref, st_ref = refs[10 + 2 * n_cast:]
    i = pl.program_id(1)

    cast_pieces = []
    for src, dst in zip(cast_in, cast_out):
        for r in range(0, src.shape[0], 2 * SUBLANES):
            def piece(src=src, dst=dst, r=r):
                dst[r:r + 2 * SUBLANES, :] = src[r:r + 2 * SUBLANES, :].astype(BF16)
            cast_pieces.append(piece)

    tq = qt_ref.shape[1]
    tk = vt_ref.shape[2]

    qt = qt_ref[...]
    row = lax.broadcasted_iota(jnp.int32, qt.shape, 0)
    zero = jnp.zeros_like(qt)
    qs_ref[:, 0:tq] = jnp.where(row < DIFF_HEAD_DIM, qt, zero)
    qs_ref[:, tq:2 * tq] = jnp.where(row >= DIFF_HEAD_DIM, qt, zero)

    nq = tq // tk
    all_units = [(t, c) for t in range(tk // ks) for c in range(2 * tq // cw)]

    def col_pos(c):
        q0 = (c * cw) % tq
        return q0 // tk, q0 % tk

    def diag_units(d):
        keep = []
        for (t, c) in all_units:
            qb, co = col_pos(c)
            if qb > d or (qb == d and t * ks < co + cw):
                keep.append((t, c))
        return keep

    def scores(j, t, c):
        k = k_ref[pl.ds(pl.multiple_of(j * tk + t * ks, ks), ks), :]
        return jnp.dot(k, qs_ref[:, c * cw:(c + 1) * cw], preferred_element_type=F32)

    def masked(t, c, st, diag):
        qb, co = col_pos(c)
        if diag is not None and qb == diag and (t + 1) * ks - 1 > co:
            key = lax.broadcasted_iota(jnp.int32, (ks, cw), 0) + t * ks
            qry = lax.broadcasted_iota(jnp.int32, (ks, cw), 1) + co
            st = jnp.where(key <= qry, st, NEG_INF)
        return st

    def pv(j, t, pt):
        vt = vt_ref[j, :, t * ks:(t + 1) * ks]
        return jnp.dot(vt, pt.astype(BF16), preferred_element_type=F32)

    def unshifted_unit(j, t, c, st, diag):
        cs = slice(c * cw, (c + 1) * cw)
        st = masked(t, c, st, diag)
        top_ref[:, cs] = jnp.maximum(top_ref[:, cs], jnp.max(st, axis=0, keepdims=True))
        acc_ref[:, cs] += pv(j, t, jnp.exp2(st))

    def running_max_unit(j, t, c, st, diag):
        cs = slice(c * cw, (c + 1) * cw)
        st = masked(t, c, st, diag)
        m_prev = m_ref[:, cs]
        m_new = jnp.maximum(m_prev, jnp.max(st, axis=0, keepdims=True))
        alpha = jnp.exp2(m_prev - m_new)
        acc_ref[:, cs] = alpha * acc_ref[:, cs] + pv(j, t, jnp.exp2(st - m_new))
        m_ref[:, cs] = m_new

    def at_slab(dj, units):
        return [(dj, t, c) for (t, c) in units]

    def attend(unit_fn, side_jobs=()):
        acc_ref[...] = jnp.zeros(acc_ref.shape, F32)

        def step(j, units, diag, next_units, extras=None):
            pending = [st_ref[d] for d in range(depth)]
            for n, (dj, t, c) in enumerate(units):
                for job in (extras[n] if extras else ()):
                    job()
                st = pending.pop(0)
                ahead = n + depth
                if ahead < len(units):
                    aj, at, ac = units[ahead]
                    pending.append(scores(j + aj, at, ac))
                elif next_units is not None:
                    aj, at, ac = next_units[ahead - len(units)]
                    st_ref[ahead - len(units)] = scores(j + aj, at, ac)
                unit_fn(j + dj, t, c, st, diag)

        per = 2 if nq % 2 == 0 else 1
        loop_units = [u for dj in range(per) for u in at_slab(dj, all_units)]

        def body(jj, carry):
            step(jj * per, loop_units, None, at_slab(per, all_units))
            return carry

        for d in range(depth):
            st_ref[d] = scores(0, *all_units[d])
        lax.fori_loop(0, i * nq // per, body, 0)
        n_diag = sum(len(diag_units(d)) for d in range(nq))
        jobs = [[] for _ in range(n_diag)]
        for n, job in enumerate(side_jobs):
            jobs[n * n_diag // len(side_jobs)].append(job)
        done = 0
        for d in range(nq):
            units = diag_units(d)
            nxt = at_slab(1, diag_units(d + 1)) if d + 1 < nq else None
            step(i * nq + d, at_slab(0, units), d, nxt, jobs[done:done + len(units)])
            done += len(units)

        lam = (jnp.exp(jnp.sum(lq1_ref[...] * lk1_ref[...], axis=-1, keepdims=True))
               - jnp.exp(jnp.sum(lq2_ref[...] * lk2_ref[...], axis=-1, keepdims=True))
               + li_ref[0])
        inv_l = 1.0 / acc_ref[HEAD_W:HEAD_W + 1, :]
        o1 = acc_ref[0:HEAD_W, 0:tq] * inv_l[:, 0:tq]
        o2 = acc_ref[0:HEAD_W, tq:2 * tq] * inv_l[:, tq:2 * tq]
        ot = o1 - lam * o2
        ms = jnp.mean(ot * ot, axis=0, keepdims=True)
        ot = ot * lax.rsqrt(ms + EPS) * w_ref[...] * (1.0 - li_ref[0])
        o_ref[...] = ot.T.astype(BF16)

    top_ref[...] = jnp.full(top_ref.shape, NEG_INF, F32)
    attend(unshifted_unit, cast_pieces)

    top = top_ref[...]
    in_range = jnp.logical_and(jnp.max(top) <= EXP2_HEADROOM, jnp.min(top) >= -EXP2_HEADROOM)

    @pl.when(jnp.logical_not(in_range))
    def _():
        m_ref[...] = jnp.full(m_ref.shape, NEG_INF, F32)
        attend(running_max_unit)


def _attn_tq(s, tk, tq=ATTN_Q):
    return max(min(tq, s), tk)


def _castable(w, n_steps):
    rows = w.shape[1]
    return rows % n_steps == 0 and (rows // n_steps) % 16 == 0


def _attention(qt, k, vt, lam_init, lq1, lk1, lq2, lk2, subln_w, n_heads, cast=(), *,
               ks=ATTN_UNIT_KEYS, cw=ATTN_UNIT_COLS, depth=ATTN_DEPTH):
    s = k.shape[0]
    tk = vt.shape[2]
    tq, ks, cw = _attn_tq(s, tk), min(ks, tk), min(cw, tk)
    assert s % tq == 0 and tq % tk == 0 and tk % ks == 0
    dh = DIFF_HEAD_DIM
    ni = s // tq
    n_steps = n_heads * ni
    lspec = pl.BlockSpec((1, dh), lambda h, i: (0, 0))
    cast_in_specs, cast_out_specs, cast_out_shapes = [], [], []
    for w, layer in cast:
        rows, cols = w.shape[1] // n_steps, w.shape[2]
        cast_in_specs.append(
            pl.BlockSpec((None, rows, cols), lambda h, i, layer=layer: (layer, h * ni + i, 0)))
        cast_out_specs.append(pl.BlockSpec((rows, cols), lambda h, i: (h * ni + i, 0)))
        cast_out_shapes.append(jax.ShapeDtypeStruct(w.shape[1:], BF16))
    outs = pl.pallas_call(
        functools.partial(_attn_kernel, ks=ks, cw=cw, depth=depth, n_cast=len(cast)),
        grid=(n_heads, ni),
        in_specs=[
            pl.BlockSpec(memory_space=pltpu.SMEM),
            pl.BlockSpec((HEAD_W, tq), lambda h, i: (h, i)),
            pl.BlockSpec((s, HEAD_W), lambda h, i: (0, h)),
            pl.BlockSpec((s // tk, VT_ROWS, tk), lambda h, i: (0, h, 0)),
            lspec, lspec, lspec, lspec,
            pl.BlockSpec((HEAD_W, 1), lambda h, i: (0, 0)),
        ] + cast_in_specs,
        out_specs=[pl.BlockSpec((tq, HEAD_W), lambda h, i: (i, h))] + cast_out_specs,
        out_shape=[jax.ShapeDtypeStruct((s, n_heads * HEAD_W), BF16)] + cast_out_shapes,
        scratch_shapes=[
            pltpu.VMEM((HEAD_W, 2 * tq), BF16),
            pltpu.VMEM((1, 2 * tq), F32),
            pltpu.VMEM((1, 2 * tq), F32),
            pltpu.VMEM((VT_ROWS, 2 * tq), F32),
            pltpu.VMEM((depth, ks, cw), F32),
        ],
        compiler_params=_params("arbitrary", "arbitrary"),
        name="diff_attn",
    )(lam_init, qt, k, vt, lq1.reshape(1, dh), lk1.reshape(1, dh),
      lq2.reshape(1, dh), lk2.reshape(1, dh), subln_w.reshape(HEAD_W, 1),
      *[w for w, _ in cast])
    return outs[0], outs[1:]


def _out_kernel(x_ref, a_ref, c_ref, wa_ref, wc_ref, o_ref):
    o_ref[...] = (x_ref[...]
                  + jnp.dot(a_ref[...], wa_ref[...], preferred_element_type=F32)
                  + jnp.dot(c_ref[...], wc_ref[...], preferred_element_type=F32))


def _out_proj(x, attn, conv, w_out, layer, *, tm=OUTPROJ_ROWS):
    s, d = x.shape
    aw, cw = attn.shape[1], conv.shape[1]
    tm = min(tm, s)
    assert s % tm == 0 and aw == cw
    return pl.pallas_call(
        _out_kernel,
        grid=(s // tm,),
        in_specs=[
            pl.BlockSpec((tm, d), lambda i: (i, 0)),
            pl.BlockSpec((tm, aw), lambda i: (i, 0)),
            pl.BlockSpec((tm, cw), lambda i: (i, 0)),
            pl.BlockSpec((None, aw, d), lambda i: (layer, 0, 0)),
            pl.BlockSpec((None, cw, d), lambda i: (layer, 1, 0)),
        ],
        out_specs=pl.BlockSpec((tm, d), lambda i: (i, 0)),
        out_shape=jax.ShapeDtypeStruct((s, d), F32),
        compiler_params=_params("parallel"),
        name="out_proj",
    )(x, attn, conv, w_out, w_out)


def _rope_lane_tables(seq):
    half = ROT_DIM // 2
    pos = jnp.arange(seq, dtype=F32)
    inv_freq = ROPE_THETA ** (-jnp.arange(0, ROT_DIM, 2, dtype=F32) / ROT_DIM)
    ang = pos[:, None] * inv_freq[None, :]
    cos, sin = jnp.cos(ang), jnp.sin(ang)
    rest = DIFF_HEAD_DIM - ROT_DIM
    ones = jnp.ones((seq, rest), F32)
    c = jnp.concatenate([cos, cos, ones], axis=-1)
    s_up = jnp.concatenate([-sin, jnp.zeros((seq, half + rest), F32)], axis=-1)
    s_dn = jnp.concatenate([jnp.zeros((seq, half), F32), sin, jnp.zeros((seq, rest), F32)], axis=-1)
    rep = LANES // DIFF_HEAD_DIM
    return jnp.tile(c, (1, rep)), jnp.tile(s_up, (1, rep)), jnp.tile(s_dn, (1, rep))


def kernel(x, norm_ffn1, ffn1_w_gate, ffn1_w_up, ffn1_w_down, norm_mix, w_in, conv_w,
           lambda_q1, lambda_k1, lambda_q2, lambda_k2, subln_w, w_out,
           norm_ffn2, ffn2_w_gate, ffn2_w_up, ffn2_w_down, norm_final):
    b, s, d = x.shape
    depth = norm_ffn1.shape[0]
    attn_w = d // 2
    n_heads = attn_w // HEAD_W
    rope_c, rope_s1, rope_s2 = _rope_lane_tables(s)
    tm_in = min(INPROJ_ROWS, s)
    n_steps = n_heads * (s // _attn_tq(s, tm_in))
    ffn1 = (ffn1_w_gate, ffn1_w_up, ffn1_w_down)
    ffn2 = (ffn2_w_gate, ffn2_w_up, ffn2_w_down)

    def bf_layer(w, l):
        return w[l].astype(BF16)[None]

    outs = []
    for bi in range(b):
        xb = x[bi]
        cur = {"ffn1": [bf_layer(w, 0) for w in ffn1], "w_in": bf_layer(w_in, 0),
               "w_out": bf_layer(w_out, 0)}
        for l in range(depth):
            lam_init = jnp.full((1,), 0.8 - 0.6 * math.exp(-0.3 * l), F32)
            xb = _ffn(xb, norm_ffn1[l], *cur["ffn1"], 0)
            qt, k, vt, conv = _inproj(xb, norm_mix[l], cur["w_in"], 0, rope_c, rope_s1, rope_s2,
                                      conv_w[l], attn_w, tm=tm_in)
            wanted = [(w, l) for w in ffn2]
            if l + 1 < depth:
                wanted += [(w, l + 1) for w in ffn1] + [(w_in, l + 1), (w_out, l + 1)]
            in_kernel = [(w, ll) for w, ll in wanted if _castable(w, n_steps)]
            attn, casted = _attention(qt, k, vt, lam_init, lambda_q1[l], lambda_k1[l],
                                      lambda_q2[l], lambda_k2[l], subln_w[l], n_heads, in_kernel)
            casted = list(casted)
            ready = [casted.pop(0)[None] if _castable(w, n_steps) else bf_layer(w, ll)
                     for w, ll in wanted]
            xb = _out_proj(xb, attn, conv, cur["w_out"], 0)
            g_final = norm_final if l == depth - 1 else None
            xb = _ffn(xb, norm_ffn2[l], *ready[:3], 0, g_final)
            if l + 1 < depth:
                cur = {"ffn1": ready[3:6], "w_in": ready[6], "w_out": ready[7]}
        outs.append(xb)
    return outs[0][None] if b == 1 else jnp.stack(outs, axis=0)
```

```python
import functools
import math

import jax
import jax.numpy as jnp
from jax import lax
from jax.experimental import pallas as pl
from jax.experimental.pallas import tpu as pltpu

F32 = jnp.float32
BF16 = jnp.bfloat16

EPS = 1e-5
NEG_INF = -1e30
ROPE_THETA = 500000.0
DIFF_HEAD_DIM = 64
HEAD_W = 2 * DIFF_HEAD_DIM
ROT_DIM = DIFF_HEAD_DIM // 4
CONV_K = 3
LOG2E = math.log2(math.e)

EXP2_HEADROOM = 60.0
VT_PAD = 16
VT_ROWS = HEAD_W + VT_PAD
LANES = 128
SUBLANES = 8
VMEM_LIMIT = 56 * 1024 * 1024

FFN_ROWS, FFN_COLS = 1024, 512
NORM_ROWS = 512
INPROJ_ROWS, INPROJ_COLS = 512, 512
OUTPROJ_ROWS = 512
ATTN_Q = 2048
ATTN_UNIT_KEYS, ATTN_UNIT_COLS = 256, 256
ATTN_DEPTH = 4


def _params(*sem):
    return pltpu.CompilerParams(dimension_semantics=sem, vmem_limit_bytes=VMEM_LIMIT)


def _rms(x, g):
    ms = jnp.mean(x * x, axis=-1, keepdims=True)
    return x * lax.rsqrt(ms + EPS) * g


def _ffn_kernel(*refs, final_norm, rows, nf):
    if final_norm:
        x_ref, g_ref, wg_ref, wu_ref, wd_ref, gf_ref, o_ref, h_ref = refs
    else:
        x_ref, g_ref, wg_ref, wu_ref, wd_ref, o_ref, h_ref = refs
    f = pl.program_id(1)
    tm = x_ref.shape[0]

    def block(first, last):
        step = rows if (first or last) else tm
        wg, wu, wd = (w[...].astype(BF16) for w in (wg_ref, wu_ref, wd_ref))
        for r in range(0, tm, step):
            rs = slice(r, r + step)
            if first:
                h = _rms(x_ref[rs, :], g_ref[...]).astype(BF16)
                h_ref[rs, :] = h
            else:
                h = h_ref[rs, :]
            gate = jnp.dot(h, wg, preferred_element_type=F32)
            up = jnp.dot(h, wu, preferred_element_type=F32)
            act = (gate * (1.0 / (1.0 + jnp.exp(-gate))) * up).astype(BF16)
            o = jnp.dot(act, wd, preferred_element_type=F32)
            if not first:
                o = o_ref[rs, :] + o
            if last:
                o = x_ref[rs, :] + 0.5 * o
                if final_norm:
                    o = _rms(o, gf_ref[...])
            o_ref[rs, :] = o

    pl.when(f == 0)(lambda: block(True, nf == 1))
    if nf > 2:
        pl.when(jnp.logical_and(f > 0, f < nf - 1))(lambda: block(False, False))
    if nf > 1:
        pl.when(f == nf - 1)(lambda: block(False, True))


def _ffn(x, g, wg, wu, wd, layer, g_final=None, *, tm=FFN_ROWS, tf=FFN_COLS):
    s, d = x.shape
    dff = wg.shape[2]
    tm, tf = min(tm, s), min(tf, dff)
    assert s % tm == 0 and dff % tf == 0
    final_norm = g_final is not None
    in_specs = [
        pl.BlockSpec((tm, d), lambda i, f: (i, 0)),
        pl.BlockSpec((1, d), lambda i, f: (0, 0)),
        pl.BlockSpec((None, d, tf), lambda i, f: (layer, 0, f)),
        pl.BlockSpec((None, d, tf), lambda i, f: (layer, 0, f)),
        pl.BlockSpec((None, tf, d), lambda i, f: (layer, f, 0)),
    ]
    args = [x, g.reshape(1, d), wg, wu, wd]
    if final_norm:
        in_specs.append(pl.BlockSpec((1, d), lambda i, f: (0, 0)))
        args.append(g_final.reshape(1, d))
    return pl.pallas_call(
        functools.partial(_ffn_kernel, final_norm=final_norm, rows=min(NORM_ROWS, tm),
                          nf=dff // tf),
        grid=(s // tm, dff // tf),
        in_specs=in_specs,
        out_specs=pl.BlockSpec((tm, d), lambda i, f: (i, 0)),
        out_shape=jax.ShapeDtypeStruct((s, d), F32),
        scratch_shapes=[pltpu.VMEM((tm, d), BF16)],
        compiler_params=_params("parallel", "arbitrary"),
        name="ffn_final" if final_norm else "ffn",
    )(*args)


def _inproj_kernel(x_ref, g_ref, w_ref, c_ref, s1_ref, s2_ref, cw_ref,
                   qt_ref, k_ref, vt_ref, conv_ref, h_ref, z_ref, *, q_scale, attn_w, tn):
    i = pl.program_id(0)
    tm = x_ref.shape[0]
    conv_w = conv_ref.shape[1]

    @pl.when(i == 0)
    def _():
        z_ref[0:SUBLANES, :] = jnp.zeros((SUBLANES, conv_w), F32)

    h_ref[...] = _rms(x_ref[...], g_ref[...]).astype(BF16)
    hn = h_ref[...]

    def proj(col):
        return jnp.dot(hn, w_ref[:, col:col + tn], preferred_element_type=F32)

    def rope(t):
        up = pltpu.roll(t, LANES - ROT_DIM // 2, axis=1)
        dn = pltpu.roll(t, ROT_DIM // 2, axis=1)
        return t * c_ref[...] + up * s1_ref[...] + dn * s2_ref[...]

    ones = jnp.ones((VT_PAD, tm), BF16)
    for n in range(3 * attn_w // tn):
        acc = proj(n * tn)
        kind, base = divmod(n * tn, attn_w)
        for c in range(tn // HEAD_W):
            t = acc[:, c * HEAD_W:(c + 1) * HEAD_W]
            head = base // HEAD_W + c
            if kind == 0:
                qt_ref[head * HEAD_W:(head + 1) * HEAD_W, :] = (rope(t) * q_scale).T.astype(BF16)
            elif kind == 1:
                k_ref[:, head * HEAD_W:(head + 1) * HEAD_W] = rope(t).astype(BF16)
            else:
                r0 = head * VT_ROWS
                vt_ref[r0:r0 + HEAD_W, :] = t.T.astype(BF16)
                vt_ref[r0 + HEAD_W:r0 + VT_ROWS, :] = ones

    for n in range(conv_w // tn):
        cs = slice(n * tn, (n + 1) * tn)
        gate_b = proj(3 * attn_w + n * tn)
        z = proj(3 * attn_w + conv_w + n * tn) * proj(3 * attn_w + 2 * conv_w + n * tn)
        z_ref[SUBLANES:SUBLANES + tm, cs] = z
        z1 = z_ref[SUBLANES - 1:SUBLANES - 1 + tm, cs]
        z2 = z_ref[SUBLANES - 2:SUBLANES - 2 + tm, cs]
        cw = cw_ref[:, cs]
        y = cw[0:1, :] * z2 + cw[1:2, :] * z1 + cw[2:3, :] * z
        conv_ref[:, cs] = (gate_b * y).astype(BF16)
        z_ref[0:SUBLANES, cs] = z_ref[tm:tm + SUBLANES, cs]


def _inproj(x, g, w_in, layer, rope_c, rope_s1, rope_s2, conv_w, attn_w, *, tm, tn=INPROJ_COLS):
    s, d = x.shape
    conv_width = conv_w.shape[1]
    tm = min(tm, s)
    tn = min(tn, attn_w, conv_width)
    assert s % tm == 0 and attn_w % tn == 0 and conv_width % tn == 0
    assert tn % HEAD_W == 0 and HEAD_W == LANES
    assert w_in.shape[2] == 3 * attn_w + 3 * conv_width
    q_scale = DIFF_HEAD_DIM ** -0.5 * LOG2E
    n_heads = attn_w // HEAD_W
    tab = pl.BlockSpec((tm, LANES), lambda i: (i, 0))
    return pl.pallas_call(
        functools.partial(_inproj_kernel, q_scale=q_scale, attn_w=attn_w, tn=tn),
        grid=(s // tm,),
        in_specs=[
            pl.BlockSpec((tm, d), lambda i: (i, 0)),
            pl.BlockSpec((1, d), lambda i: (0, 0)),
            pl.BlockSpec((None, d, w_in.shape[2]), lambda i: (layer, 0, 0),
                         pipeline_mode=pl.Buffered(1)),
            tab, tab, tab,
            pl.BlockSpec((CONV_K, conv_width), lambda i: (0, 0)),
        ],
        out_specs=[
            pl.BlockSpec((attn_w, tm), lambda i: (0, i)),
            pl.BlockSpec((tm, attn_w), lambda i: (i, 0)),
            pl.BlockSpec((None, n_heads * VT_ROWS, tm), lambda i: (i, 0, 0)),
            pl.BlockSpec((tm, conv_width), lambda i: (i, 0)),
        ],
        out_shape=[
            jax.ShapeDtypeStruct((attn_w, s), BF16),
            jax.ShapeDtypeStruct((s, attn_w), BF16),
            jax.ShapeDtypeStruct((s // tm, n_heads * VT_ROWS, tm), BF16),
            jax.ShapeDtypeStruct((s, conv_width), BF16),
        ],
        scratch_shapes=[
            pltpu.VMEM((tm, d), BF16),
            pltpu.VMEM((tm + SUBLANES, conv_width), F32),
        ],
        compiler_params=_params("arbitrary"),
        name="in_proj",
    )(x, g.reshape(1, d), w_in, rope_c, rope_s1, rope_s2, conv_w)


def _attn_kernel(*refs, ks, cw, depth, n_cast):
    (li_ref, qt_ref, k_ref, vt_ref, lq1_ref, lk1_ref, lq2_ref, lk2_ref, w_ref) = refs[:9]
    cast_in = refs[9:9 + n_cast]
    o_ref = refs[9 + n_cast]
    cast_out = refs[10 + n_cast:10 + 2 * n_cast]
    qs_ref, m_ref, top_ref, acc_ref, st_ref = refs[10 + 2 * n_cast:]
    i = pl.program_id(1)

    cast_pieces = []
    for src, dst in zip(cast_in, cast_out):
        for r in range(0, src.shape[0], 2 * SUBLANES):
            def piece(src=src, dst=dst, r=r):
                dst[r:r + 2 * SUBLANES, :] = src[r:r + 2 * SUBLANES, :].astype(BF16)
            cast_pieces.append(piece)

    tq = qt_ref.shape[1]
    tk = vt_ref.shape[2]

    qt = qt_ref[...]
    row = lax.broadcasted_iota(jnp.int32, qt.shape, 0)
    zero = jnp.zeros_like(qt)
    qs_ref[:, 0:tq] = jnp.where(row < DIFF_HEAD_DIM, qt, zero)
    qs_ref[:, tq:2 * tq] = jnp.where(row >= DIFF_HEAD_DIM, qt, zero)

    nq = tq // tk
    all_units = [(t, c) for t in range(tk // ks) for c in range(2 * tq // cw)]

    def col_pos(c):
        q0 = (c * cw) % tq
        return q0 // tk, q0 % tk

    def diag_units(d):
        keep = []
        for (t, c) in all_units:
            qb, co = col_pos(c)
            if qb > d or (qb == d and t * ks < co + cw):
                keep.append((t, c))
        return keep

    def scores(j, t, c):
        k = k_ref[pl.ds(pl.multiple_of(j * tk + t * ks, ks), ks), :]
        return jnp.dot(k, qs_ref[:, c * cw:(c + 1) * cw], preferred_element_type=F32)

    def masked(t, c, st, diag):
        qb, co = col_pos(c)
        if diag is not None and qb == diag and (t + 1) * ks - 1 > co:
            key = lax.broadcasted_iota(jnp.int32, (ks, cw), 0) + t * ks
            qry = lax.broadcasted_iota(jnp.int32, (ks, cw), 1) + co
            st = jnp.where(key <= qry, st, NEG_INF)
        return st

    def pv(j, t, pt):
        vt = vt_ref[j, :, t * ks:(t + 1) * ks]
        return jnp.dot(vt, pt.astype(BF16), preferred_element_type=F32)

    def unshifted_unit(j, t, c, st, diag):
        cs = slice(c * cw, (c + 1) * cw)
        st = masked(t, c, st, diag)
        top_ref[:, cs] = jnp.maximum(top_ref[:, cs], jnp.max(st, axis=0, keepdims=True))
        acc_ref[:, cs] += pv(j, t, jnp.exp2(st))

    def running_max_unit(j, t, c, st, diag):
        cs = slice(c * cw, (c + 1) * cw)
        st = masked(t, c, st, diag)
        m_prev = m_ref[:, cs]
        m_new = jnp.maximum(m_prev, jnp.max(st, axis=0, keepdims=True))
        alpha = jnp.exp2(m_prev - m_new)
        acc_ref[:, cs] = alpha * acc_ref[:, cs] + pv(j, t, jnp.exp2(st - m_new))
        m_ref[:, cs] = m_new

    def at_slab(dj, units):
        return [(dj, t, c) for (t, c) in units]

    def attend(unit_fn, side_jobs=()):
        acc_ref[...] = jnp.zeros(acc_ref.shape, F32)

        def step(j, units, diag, next_units, extras=None):
            pending = [st_ref[d] for d in range(depth)]
            for n, (dj, t, c) in enumerate(units):
                for job in (extras[n] if extras else ()):
                    job()
                st = pending.pop(0)
                ahead = n + depth
                if ahead < len(units):
                    aj, at, ac = units[ahead]
                    pending.append(scores(j + aj, at, ac))
                elif next_units is not None:
                    aj, at, ac = next_units[ahead - len(units)]
                    st_ref[ahead - len(units)] = scores(j + aj, at, ac)
                unit_fn(j + dj, t, c, st, diag)

        per = 2 if nq % 2 == 0 else 1
        loop_units = [u for dj in range(per) for u in at_slab(dj, all_units)]

        def body(jj, carry):
            step(jj * per, loop_units, None, at_slab(per, all_units))
            return carry

        for d in range(depth):
            st_ref[d] = scores(0, *all_units[d])
        lax.fori_loop(0, i * nq // per, body, 0)
        n_diag = sum(len(diag_units(d)) for d in range(nq))
        jobs = [[] for _ in range(n_diag)]
        for n, job in enumerate(side_jobs):
            jobs[n * n_diag // len(side_jobs)].append(job)
        done = 0
        for d in range(nq):
            units = diag_units(d)
            nxt = at_slab(1, diag_units(d + 1)) if d + 1 < nq else None
            step(i * nq + d, at_slab(0, units), d, nxt, jobs[done:done + len(units)])
            done += len(units)

        lam = (jnp.exp(jnp.sum(lq1_ref[...] * lk1_ref[...], axis=-1, keepdims=True))
               - jnp.exp(jnp.sum(lq2_ref[...] * lk2_ref[...], axis=-1, keepdims=True))
               + li_ref[0])
        inv_l = 1.0 / acc_ref[HEAD_W:HEAD_W + 1, :]
        o1 = acc_ref[0:HEAD_W, 0:tq] * inv_l[:, 0:tq]
        o2 = acc_ref[0:HEAD_W, tq:2 * tq] * inv_l[:, tq:2 * tq]
        ot = o1 - lam * o2
        ms = jnp.mean(ot * ot, axis=0, keepdims=True)
        ot = ot * lax.rsqrt(ms + EPS) * w_ref[...] * (1.0 - li_ref[0])
        o_ref[...] = ot.T.astype(BF16)

    top_ref[...] = jnp.full(top_ref.shape, NEG_INF, F32)
    attend(unshifted_unit, cast_pieces)

    top = top_ref[...]
    in_range = jnp.logical_and(jnp.max(top) <= EXP2_HEADROOM, jnp.min(top) >= -EXP2_HEADROOM)

    @pl.when(jnp.logical_not(in_range))
    def _():
        m_ref[...] = jnp.full(m_ref.shape, NEG_INF, F32)
        attend(running_max_unit)


def _attn_tq(s, tk, tq=ATTN_Q):
    return max(min(tq, s), tk)


def _castable(w, n_steps):
    rows = w.shape[1]
    return rows % n_steps == 0 and (rows // n_steps) % 16 == 0


def _attention(qt, k, vt, lam_init, lq1, lk1, lq2, lk2, subln_w, n_heads, cast=(), *,
               ks=ATTN_UNIT_KEYS, cw=ATTN_UNIT_COLS, depth=ATTN_DEPTH):
    s = k.shape[0]
    tk = vt.shape[2]
    tq, ks, cw = _attn_tq(s, tk), min(ks, tk), min(cw, tk)
    assert s % tq == 0 and tq % tk == 0 and tk % ks == 0
    dh = DIFF_HEAD_DIM
    ni = s // tq
    n_steps = n_heads * ni
    lspec = pl.BlockSpec((1, dh), lambda h, i: (0, 0))
    cast_in_specs, cast_out_specs, cast_out_shapes = [], [], []
    for w, layer in cast:
        rows, cols = w.shape[1] // n_steps, w.shape[2]
        cast_in_specs.append(
            pl.BlockSpec((None, rows, cols), lambda h, i, layer=layer: (layer, h * ni + i, 0)))
        cast_out_specs.append(pl.BlockSpec((rows, cols), lambda h, i: (h * ni + i, 0)))
        cast_out_shapes.append(jax.ShapeDtypeStruct(w.shape[1:], BF16))
    outs = pl.pallas_call(
        functools.partial(_attn_kernel, ks=ks, cw=cw, depth=depth, n_cast=len(cast)),
        grid=(n_heads, ni),
        in_specs=[
            pl.BlockSpec(memory_space=pltpu.SMEM),
            pl.BlockSpec((HEAD_W, tq), lambda h, i: (h, i)),
            pl.BlockSpec((s, HEAD_W), lambda h, i: (0, h)),
            pl.BlockSpec((s // tk, VT_ROWS, tk), lambda h, i: (0, h, 0)),
            lspec, lspec, lspec, lspec,
            pl.BlockSpec((HEAD_W, 1), lambda h, i: (0, 0)),
        ] + cast_in_specs,
        out_specs=[pl.BlockSpec((tq, HEAD_W), lambda h, i: (i, h))] + cast_out_specs,
        out_shape=[jax.ShapeDtypeStruct((s, n_heads * HEAD_W), BF16)] + cast_out_shapes,
        scratch_shapes=[
            pltpu.VMEM((HEAD_W, 2 * tq), BF16),
            pltpu.VMEM((1, 2 * tq), F32),
            pltpu.VMEM((1, 2 * tq), F32),
            pltpu.VMEM((VT_ROWS, 2 * tq), F32),
            pltpu.VMEM((depth, ks, cw), F32),
        ],
        compiler_params=_params("arbitrary", "arbitrary"),
        name="diff_attn",
    )(lam_init, qt, k, vt, lq1.reshape(1, dh), lk1.reshape(1, dh),
      lq2.reshape(1, dh), lk2.reshape(1, dh), subln_w.reshape(HEAD_W, 1),
      *[w for w, _ in cast])
    return outs[0], outs[1:]


def _out_kernel(x_ref, a_ref, c_ref, wa_ref, wc_ref, o_ref):
    o_ref[...] = (x_ref[...]
                  + jnp.dot(a_ref[...], wa_ref[...], preferred_element_type=F32)
                  + jnp.dot(c_ref[...], wc_ref[...], preferred_element_type=F32))


def _out_proj(x, attn, conv, w_out, layer, *, tm=OUTPROJ_ROWS):
    s, d = x.shape
    aw, cw = attn.shape[1], conv.shape[1]
    tm = min(tm, s)
    assert s % tm == 0 and aw == cw
    return pl.pallas_call(
        _out_kernel,
        grid=(s // tm,),
        in_specs=[
            pl.BlockSpec((tm, d), lambda i: (i, 0)),
            pl.BlockSpec((tm, aw), lambda i: (i, 0)),
            pl.BlockSpec((tm, cw), lambda i: (i, 0)),
            pl.BlockSpec((None, aw, d), lambda i: (layer, 0, 0)),
            pl.BlockSpec((None, cw, d), lambda i: (layer, 1, 0)),
        ],
        out_specs=pl.BlockSpec((tm, d), lambda i: (i, 0)),
        out_shape=jax.ShapeDtypeStruct((s, d), F32),
        compiler_params=_params("parallel"),
        name="out_proj",
    )(x, attn, conv, w_out, w_out)


def _rope_lane_tables(seq):
    half = ROT_DIM // 2
    pos = jnp.arange(seq, dtype=F32)
    inv_freq = ROPE_THETA ** (-jnp.arange(0, ROT_DIM, 2, dtype=F32) / ROT_DIM)
    ang = pos[:, None] * inv_freq[None, :]
    cos, sin = jnp.cos(ang), jnp.sin(ang)
    rest = DIFF_HEAD_DIM - ROT_DIM
    ones = jnp.ones((seq, rest), F32)
    c = jnp.concatenate([cos, cos, ones], axis=-1)
    s_up = jnp.concatenate([-sin, jnp.zeros((seq, half + rest), F32)], axis=-1)
    s_dn = jnp.concatenate([jnp.zeros((seq, half), F32), sin, jnp.zeros((seq, rest), F32)], axis=-1)
    rep = LANES // DIFF_HEAD_DIM
    return jnp.tile(c, (1, rep)), jnp.tile(s_up, (1, rep)), jnp.tile(s_dn, (1, rep))


def kernel(x, norm_ffn1, ffn1_w_gate, ffn1_w_up, ffn1_w_down, norm_mix, w_in, conv_w,
           lambda_q1, lambda_k1, lambda_q2, lambda_k2, subln_w, w_out,
           norm_ffn2, ffn2_w_gate, ffn2_w_up, ffn2_w_down, norm_final):
    b, s, d = x.shape
    depth = norm_ffn1.shape[0]
    attn_w = d // 2
    n_heads = attn_w // HEAD_W
    rope_c, rope_s1, rope_s2 = _rope_lane_tables(s)
    tm_in = min(INPROJ_ROWS, s)
    n_steps = n_heads * (s // _attn_tq(s, tm_in))
    ffn1 = (ffn1_w_gate, ffn1_w_up, ffn1_w_down)
    ffn2 = (ffn2_w_gate, ffn2_w_up, ffn2_w_down)

    def bf_layer(w, l):
        return w[l].astype(BF16)[None]

    outs = []
    for bi in range(b):
        xb = x[bi]
        cur = {"ffn1": None, "w_in": bf_layer(w_in, 0), "w_out": bf_layer(w_out, 0)}
        for l in range(depth):
            lam_init = jnp.full((1,), 0.8 - 0.6 * math.exp(-0.3 * l), F32)
            if cur["ffn1"] is None:
                xb = _ffn(xb, norm_ffn1[l], *ffn1, 0, tf=FFN_COLS // 2)
            else:
                xb = _ffn(xb, norm_ffn1[l], *cur["ffn1"], 0)
            qt, k, vt, conv = _inproj(xb, norm_mix[l], cur["w_in"], 0, rope_c, rope_s1, rope_s2,
                                      conv_w[l], attn_w, tm=tm_in)
            wanted = [(w, l) for w in ffn2]
            if l + 1 < depth:
                wanted += [(w, l + 1) for w in ffn1] + [(w_in, l + 1), (w_out, l + 1)]
            in_kernel = [(w, ll) for w, ll in wanted if _castable(w, n_steps)]
            attn, casted = _attention(qt, k, vt, lam_init, lambda_q1[l], lambda_k1[l],
                                      lambda_q2[l], lambda_k2[l], subln_w[l], n_heads, in_kernel)
            casted = list(casted)
            ready = [casted.pop(0)[None] if _castable(w, n_steps) else bf_layer(w, ll)
                     for w, ll in wanted]
            xb = _out_proj(xb, attn, conv, cur["w_out"], 0)
            g_final = norm_final if l == depth - 1 else None
            xb = _ffn(xb, norm_ffn2[l], *ready[:3], 0, g_final)
            if l + 1 < depth:
                cur = {"ffn1": ready[3:6], "w_in": ready[6], "w_out": ready[7]}
        outs.append(xb)
    return outs[0][None] if b == 1 else jnp.stack(outs, axis=0)
```

```python
import functools
import math

import jax
import jax.numpy as jnp
from jax import lax
from jax.experimental import pallas as pl
from jax.experimental.pallas import tpu as pltpu

F32 = jnp.float32
BF16 = jnp.bfloat16

EPS = 1e-5
NEG_INF = -1e30
ROPE_THETA = 500000.0
DIFF_HEAD_DIM = 64
HEAD_W = 2 * DIFF_HEAD_DIM
ROT_DIM = DIFF_HEAD_DIM // 4
CONV_K = 3
LOG2E = math.log2(math.e)

EXP2_HEADROOM = 60.0
LANES = 128
SUBLANES = 8
VMEM_LIMIT = 56 * 1024 * 1024

FFN_ROWS, FFN_COLS = 1024, 512
NORM_ROWS = 512
INPROJ_ROWS, INPROJ_COLS = 512, 512
OUTPROJ_ROWS = 512
ATTN_Q = 2048
ATTN_UNIT_KEYS, ATTN_UNIT_COLS = 256, 256
ATTN_DEPTH = 4


def _params(*sem):
    return pltpu.CompilerParams(dimension_semantics=sem, vmem_limit_bytes=VMEM_LIMIT)


def _rms(x, g):
    ms = jnp.mean(x * x, axis=-1, keepdims=True)
    return x * lax.rsqrt(ms + EPS) * g


def _ffn_kernel(*refs, final_norm, rows, nf):
    if final_norm:
        x_ref, g_ref, wg_ref, wu_ref, wd_ref, gf_ref, o_ref, h_ref = refs
    else:
        x_ref, g_ref, wg_ref, wu_ref, wd_ref, o_ref, h_ref = refs
    f = pl.program_id(1)
    tm = x_ref.shape[0]

    def block(first, last):
        step = rows if (first or last) else tm
        for r in range(0, tm, step):
            rs = slice(r, r + step)
            if first:
                h = _rms(x_ref[rs, :], g_ref[...]).astype(BF16)
                h_ref[rs, :] = h
            else:
                h = h_ref[rs, :]
            gate = jnp.dot(h, wg_ref[...], preferred_element_type=F32)
            up = jnp.dot(h, wu_ref[...], preferred_element_type=F32)
            act = (gate * (1.0 / (1.0 + jnp.exp(-gate))) * up).astype(BF16)
            o = jnp.dot(act, wd_ref[...], preferred_element_type=F32)
            if not first:
                o = o_ref[rs, :] + o
            if last:
                o = x_ref[rs, :] + 0.5 * o
                if final_norm:
                    o = _rms(o, gf_ref[...])
            o_ref[rs, :] = o

    pl.when(f == 0)(lambda: block(True, nf == 1))
    if nf > 2:
        pl.when(jnp.logical_and(f > 0, f < nf - 1))(lambda: block(False, False))
    if nf > 1:
        pl.when(f == nf - 1)(lambda: block(False, True))


def _ffn(x, g, wg, wu, wd, layer, g_final=None, *, tm=FFN_ROWS, tf=FFN_COLS):
    s, d = x.shape
    dff = wg.shape[2]
    tm, tf = min(tm, s), min(tf, dff)
    assert s % tm == 0 and dff % tf == 0
    final_norm = g_final is not None
    in_specs = [
        pl.BlockSpec((tm, d), lambda i, f: (i, 0)),
        pl.BlockSpec((1, d), lambda i, f: (0, 0)),
        pl.BlockSpec((None, d, tf), lambda i, f: (layer, 0, f)),
        pl.BlockSpec((None, d, tf), lambda i, f: (layer, 0, f)),
        pl.BlockSpec((None, tf, d), lambda i, f: (layer, f, 0)),
    ]
    args = [x, g.reshape(1, d), wg, wu, wd]
    if final_norm:
        in_specs.append(pl.BlockSpec((1, d), lambda i, f: (0, 0)))
        args.append(g_final.reshape(1, d))
    return pl.pallas_call(
        functools.partial(_ffn_kernel, final_norm=final_norm, rows=min(NORM_ROWS, tm),
                          nf=dff // tf),
        grid=(s // tm, dff // tf),
        in_specs=in_specs,
        out_specs=pl.BlockSpec((tm, d), lambda i, f: (i, 0)),
        out_shape=jax.ShapeDtypeStruct((s, d), F32),
        scratch_shapes=[pltpu.VMEM((tm, d), BF16)],
        compiler_params=_params("parallel", "arbitrary"),
        name="ffn_final" if final_norm else "ffn",
    )(*args)


def _inproj_kernel(x_ref, g_ref, w_ref, c_ref, s1_ref, s2_ref, cw_ref,
                   qt_ref, k_ref, vt_ref, conv_ref, h_ref, z_ref, *, q_scale, attn_w, tn):
    i = pl.program_id(0)
    tm = x_ref.shape[0]
    conv_w = conv_ref.shape[1]

    @pl.when(i == 0)
    def _():
        z_ref[0:SUBLANES, :] = jnp.zeros((SUBLANES, conv_w), F32)

    h_ref[...] = _rms(x_ref[...], g_ref[...]).astype(BF16)
    hn = h_ref[...]

    def proj(col):
        return jnp.dot(hn, w_ref[:, col:col + tn], preferred_element_type=F32)

    def rope(t):
        up = pltpu.roll(t, LANES - ROT_DIM // 2, axis=1)
        dn = pltpu.roll(t, ROT_DIM // 2, axis=1)
        return t * c_ref[...] + up * s1_ref[...] + dn * s2_ref[...]

    for n in range(3 * attn_w // tn):
        acc = proj(n * tn)
        kind, base = divmod(n * tn, attn_w)
        for c in range(tn // HEAD_W):
            t = acc[:, c * HEAD_W:(c + 1) * HEAD_W]
            head = base // HEAD_W + c
            if kind == 0:
                qt_ref[head * HEAD_W:(head + 1) * HEAD_W, :] = (rope(t) * q_scale).T.astype(BF16)
            elif kind == 1:
                k_ref[:, head * HEAD_W:(head + 1) * HEAD_W] = rope(t).astype(BF16)
            else:
                vt_ref[head * HEAD_W:(head + 1) * HEAD_W, :] = t.T.astype(BF16)

    for n in range(conv_w // tn):
        cs = slice(n * tn, (n + 1) * tn)
        gate_b = proj(3 * attn_w + n * tn)
        z = proj(3 * attn_w + conv_w + n * tn) * proj(3 * attn_w + 2 * conv_w + n * tn)
        z_ref[SUBLANES:SUBLANES + tm, cs] = z
        z1 = z_ref[SUBLANES - 1:SUBLANES - 1 + tm, cs]
        z2 = z_ref[SUBLANES - 2:SUBLANES - 2 + tm, cs]
        cw = cw_ref[:, cs]
        y = cw[0:1, :] * z2 + cw[1:2, :] * z1 + cw[2:3, :] * z
        conv_ref[:, cs] = (gate_b * y).astype(BF16)
        z_ref[0:SUBLANES, cs] = z_ref[tm:tm + SUBLANES, cs]


def _inproj(x, g, w_in, layer, rope_c, rope_s1, rope_s2, conv_w, attn_w, *, tm, tn=INPROJ_COLS):
    s, d = x.shape
    conv_width = conv_w.shape[1]
    tm = min(tm, s)
    tn = min(tn, attn_w, conv_width)
    assert s % tm == 0 and attn_w % tn == 0 and conv_width % tn == 0
    assert tn % HEAD_W == 0 and HEAD_W == LANES
    assert w_in.shape[2] == 3 * attn_w + 3 * conv_width
    q_scale = DIFF_HEAD_DIM ** -0.5 * LOG2E
    tab = pl.BlockSpec((tm, LANES), lambda i: (i, 0))
    return pl.pallas_call(
        functools.partial(_inproj_kernel, q_scale=q_scale, attn_w=attn_w, tn=tn),
        grid=(s // tm,),
        in_specs=[
            pl.BlockSpec((tm, d), lambda i: (i, 0)),
            pl.BlockSpec((1, d), lambda i: (0, 0)),
            pl.BlockSpec((None, d, w_in.shape[2]), lambda i: (layer, 0, 0),
                         pipeline_mode=pl.Buffered(1)),
            tab, tab, tab,
            pl.BlockSpec((CONV_K, conv_width), lambda i: (0, 0)),
        ],
        out_specs=[
            pl.BlockSpec((attn_w, tm), lambda i: (0, i)),
            pl.BlockSpec((tm, attn_w), lambda i: (i, 0)),
            pl.BlockSpec((None, attn_w, tm), lambda i: (i, 0, 0)),
            pl.BlockSpec((tm, conv_width), lambda i: (i, 0)),
        ],
        out_shape=[
            jax.ShapeDtypeStruct((attn_w, s), BF16),
            jax.ShapeDtypeStruct((s, attn_w), BF16),
            jax.ShapeDtypeStruct((s // tm, attn_w, tm), BF16),
            jax.ShapeDtypeStruct((s, conv_width), BF16),
        ],
        scratch_shapes=[
            pltpu.VMEM((tm, d), BF16),
            pltpu.VMEM((tm + SUBLANES, conv_width), F32),
        ],
        compiler_params=_params("arbitrary"),
        name="in_proj",
    )(x, g.reshape(1, d), w_in, rope_c, rope_s1, rope_s2, conv_w)


def _attn_kernel(*refs, ks, cw, depth, n_cast):
    (li_ref, qt_ref, k_ref, vt_ref, lq1_ref, lk1_ref, lq2_ref, lk2_ref, w_ref) = refs[:9]
    cast_in = refs[9:9 + n_cast]
    o_ref = refs[9 + n_cast]
    cast_out = refs[10 + n_cast:10 + 2 * n_cast]
    qs_ref, m_ref, top_ref, l_ref, acc_ref, st_ref = refs[10 + 2 * n_cast:]
    i = pl.program_id(1)

    cast_pieces = []
    for src, dst in zip(cast_in, cast_out):
        for r in range(0, src.shape[0], 2 * SUBLANES):
            def piece(src=src, dst=dst, r=r):
                dst[r:r + 2 * SUBLANES, :] = src[r:r + 2 * SUBLANES, :].astype(BF16)
            cast_pieces.append(piece)

    tq = qt_ref.shape[1]
    tk = vt_ref.shape[2]

    qt = qt_ref[...]
    row = lax.broadcasted_iota(jnp.int32, qt.shape, 0)
    zero = jnp.zeros_like(qt)
    qs_ref[:, 0:tq] = jnp.where(row < DIFF_HEAD_DIM, qt, zero)
    qs_ref[:, tq:2 * tq] = jnp.where(row >= DIFF_HEAD_DIM, qt, zero)

    nq = tq // tk
    all_units = [(t, c) for t in range(tk // ks) for c in range(2 * tq // cw)]

    def col_pos(c):
        q0 = (c * cw) % tq
        return q0 // tk, q0 % tk

    def diag_units(d):
        keep = []
        for (t, c) in all_units:
            qb, co = col_pos(c)
            if qb > d or (qb == d and t * ks < co + cw):
                keep.append((t, c))
        return keep

    def scores(j, t, c):
        k = k_ref[pl.ds(pl.multiple_of(j * tk + t * ks, ks), ks), :]
        return jnp.dot(k, qs_ref[:, c * cw:(c + 1) * cw], preferred_element_type=F32)

    def masked(t, c, st, diag):
        qb, co = col_pos(c)
        if diag is not None and qb == diag and (t + 1) * ks - 1 > co:
            key = lax.broadcasted_iota(jnp.int32, (ks, cw), 0) + t * ks
            qry = lax.broadcasted_iota(jnp.int32, (ks, cw), 1) + co
            st = jnp.where(key <= qry, st, NEG_INF)
        return st

    def pv(j, t, pt):
        vt = vt_ref[j, :, t * ks:(t + 1) * ks]
        return jnp.dot(vt, pt.astype(BF16), preferred_element_type=F32)

    def unshifted_unit(j, t, c, st, diag):
        cs = slice(c * cw, (c + 1) * cw)
        st = masked(t, c, st, diag)
        top_ref[:, cs] = jnp.maximum(top_ref[:, cs], jnp.max(st, axis=0, keepdims=True))
        pt = jnp.exp2(st)
        l_ref[:, cs] += jnp.sum(pt, axis=0, keepdims=True)
        acc_ref[:, cs] += pv(j, t, pt)

    def running_max_unit(j, t, c, st, diag):
        cs = slice(c * cw, (c + 1) * cw)
        st = masked(t, c, st, diag)
        m_prev = m_ref[:, cs]
        m_new = jnp.maximum(m_prev, jnp.max(st, axis=0, keepdims=True))
        alpha = jnp.exp2(m_prev - m_new)
        pt = jnp.exp2(st - m_new)
        l_ref[:, cs] = alpha * l_ref[:, cs] + jnp.sum(pt, axis=0, keepdims=True)
        acc_ref[:, cs] = alpha * acc_ref[:, cs] + pv(j, t, pt)
        m_ref[:, cs] = m_new

    def at_slab(dj, units):
        return [(dj, t, c) for (t, c) in units]

    def attend(unit_fn, side_jobs=()):
        acc_ref[...] = jnp.zeros(acc_ref.shape, F32)
        l_ref[...] = jnp.zeros(l_ref.shape, F32)

        def step(j, units, diag, next_units, extras=None):
            pending = [st_ref[d] for d in range(depth)]
            for n, (dj, t, c) in enumerate(units):
                for job in (extras[n] if extras else ()):
                    job()
                st = pending.pop(0)
                ahead = n + depth
                if ahead < len(units):
                    aj, at, ac = units[ahead]
                    pending.append(scores(j + aj, at, ac))
                elif next_units is not None:
                    aj, at, ac = next_units[ahead - len(units)]
                    st_ref[ahead - len(units)] = scores(j + aj, at, ac)
                unit_fn(j + dj, t, c, st, diag)

        per = 2 if nq % 2 == 0 else 1
        loop_units = [u for dj in range(per) for u in at_slab(dj, all_units)]

        def body(jj, carry):
            step(jj * per, loop_units, None, at_slab(per, all_units))
            return carry

        for d in range(depth):
            st_ref[d] = scores(0, *all_units[d])
        lax.fori_loop(0, i * nq // per, body, 0)
        n_diag = sum(len(diag_units(d)) for d in range(nq))
        jobs = [[] for _ in range(n_diag)]
        for n, job in enumerate(side_jobs):
            jobs[n * n_diag // len(side_jobs)].append(job)
        done = 0
        for d in range(nq):
            units = diag_units(d)
            nxt = at_slab(1, diag_units(d + 1)) if d + 1 < nq else None
            step(i * nq + d, at_slab(0, units), d, nxt, jobs[done:done + len(units)])
            done += len(units)

        lam = (jnp.exp(jnp.sum(lq1_ref[...] * lk1_ref[...], axis=-1, keepdims=True))
               - jnp.exp(jnp.sum(lq2_ref[...] * lk2_ref[...], axis=-1, keepdims=True))
               + li_ref[0])
        inv_l = 1.0 / l_ref[...]
        o1 = acc_ref[:, 0:tq] * inv_l[:, 0:tq]
        o2 = acc_ref[:, tq:2 * tq] * inv_l[:, tq:2 * tq]
        ot = o1 - lam * o2
        ms = jnp.mean(ot * ot, axis=0, keepdims=True)
        ot = ot * lax.rsqrt(ms + EPS) * w_ref[...] * (1.0 - li_ref[0])
        o_ref[...] = ot.T.astype(BF16)

    top_ref[...] = jnp.full(top_ref.shape, NEG_INF, F32)
    attend(unshifted_unit, cast_pieces)

    top = top_ref[...]
    in_range = jnp.logical_and(jnp.max(top) <= EXP2_HEADROOM, jnp.min(top) >= -EXP2_HEADROOM)

    @pl.when(jnp.logical_not(in_range))
    def _():
        m_ref[...] = jnp.full(m_ref.shape, NEG_INF, F32)
        attend(running_max_unit)


def _attn_tq(s, tk, tq=ATTN_Q):
    return max(min(tq, s), tk)


def _castable(w, n_steps):
    rows = w.shape[1]
    return rows % n_steps == 0 and (rows // n_steps) % 16 == 0


def _attention(qt, k, vt, lam_init, lq1, lk1, lq2, lk2, subln_w, n_heads, cast=(), *,
               ks=ATTN_UNIT_KEYS, cw=ATTN_UNIT_COLS, depth=ATTN_DEPTH):
    s = k.shape[0]
    tk = vt.shape[2]
    tq, ks, cw = _attn_tq(s, tk), min(ks, tk), min(cw, tk)
    assert s % tq == 0 and tq % tk == 0 and tk % ks == 0
    dh = DIFF_HEAD_DIM
    ni = s // tq
    n_steps = n_heads * ni
    lspec = pl.BlockSpec((1, dh), lambda h, i: (0, 0))
    cast_in_specs, cast_out_specs, cast_out_shapes = [], [], []
    for w, layer in cast:
        rows, cols = w.shape[1] // n_steps, w.shape[2]
        cast_in_specs.append(
            pl.BlockSpec((None, rows, cols), lambda h, i, layer=layer: (layer, h * ni + i, 0)))
        cast_out_specs.append(pl.BlockSpec((rows, cols), lambda h, i: (h * ni + i, 0)))
        cast_out_shapes.append(jax.ShapeDtypeStruct(w.shape[1:], BF16))
    outs = pl.pallas_call(
        functools.partial(_attn_kernel, ks=ks, cw=cw, depth=depth, n_cast=len(cast)),
        grid=(n_heads, ni),
        in_specs=[
            pl.BlockSpec(memory_space=pltpu.SMEM),
            pl.BlockSpec((HEAD_W, tq), lambda h, i: (h, i)),
            pl.BlockSpec((s, HEAD_W), lambda h, i: (0, h)),
            pl.BlockSpec((s // tk, HEAD_W, tk), lambda h, i: (0, h, 0)),
            lspec, lspec, lspec, lspec,
            pl.BlockSpec((HEAD_W, 1), lambda h, i: (0, 0)),
        ] + cast_in_specs,
        out_specs=[pl.BlockSpec((tq, HEAD_W), lambda h, i: (i, h))] + cast_out_specs,
        out_shape=[jax.ShapeDtypeStruct((s, n_heads * HEAD_W), BF16)] + cast_out_shapes,
        scratch_shapes=[
            pltpu.VMEM((HEAD_W, 2 * tq), BF16),
            pltpu.VMEM((1, 2 * tq), F32),
            pltpu.VMEM((1, 2 * tq), F32),
            pltpu.VMEM((1, 2 * tq), F32),
            pltpu.VMEM((HEAD_W, 2 * tq), F32),
            pltpu.VMEM((depth, ks, cw), F32),
        ],
        compiler_params=_params("arbitrary", "arbitrary"),
        name="diff_attn",
    )(lam_init, qt, k, vt, lq1.reshape(1, dh), lk1.reshape(1, dh),
      lq2.reshape(1, dh), lk2.reshape(1, dh), subln_w.reshape(HEAD_W, 1),
      *[w for w, _ in cast])
    return outs[0], outs[1:]


def _out_kernel(x_ref, a_ref, c_ref, wa_ref, wc_ref, o_ref):
    o_ref[...] = (x_ref[...]
                  + jnp.dot(a_ref[...], wa_ref[...], preferred_element_type=F32)
                  + jnp.dot(c_ref[...], wc_ref[...], preferred_element_type=F32))


def _out_proj(x, attn, conv, w_out, layer, *, tm=OUTPROJ_ROWS):
    s, d = x.shape
    aw, cw = attn.shape[1], conv.shape[1]
    tm = min(tm, s)
    assert s % tm == 0 and aw == cw
    return pl.pallas_call(
        _out_kernel,
        grid=(s // tm,),
        in_specs=[
            pl.BlockSpec((tm, d), lambda i: (i, 0)),
            pl.BlockSpec((tm, aw), lambda i: (i, 0)),
            pl.BlockSpec((tm, cw), lambda i: (i, 0)),
            pl.BlockSpec((None, aw, d), lambda i: (layer, 0, 0)),
            pl.BlockSpec((None, cw, d), lambda i: (layer, 1, 0)),
        ],
        out_specs=pl.BlockSpec((tm, d), lambda i: (i, 0)),
        out_shape=jax.ShapeDtypeStruct((s, d), F32),
        compiler_params=_params("parallel"),
        name="out_proj",
    )(x, attn, conv, w_out, w_out)


def _rope_lane_tables(seq):
    half = ROT_DIM // 2
    pos = jnp.arange(seq, dtype=F32)
    inv_freq = ROPE_THETA ** (-jnp.arange(0, ROT_DIM, 2, dtype=F32) / ROT_DIM)
    ang = pos[:, None] * inv_freq[None, :]
    cos, sin = jnp.cos(ang), jnp.sin(ang)
    rest = DIFF_HEAD_DIM - ROT_DIM
    ones = jnp.ones((seq, rest), F32)
    c = jnp.concatenate([cos, cos, ones], axis=-1)
    s_up = jnp.concatenate([-sin, jnp.zeros((seq, half + rest), F32)], axis=-1)
    s_dn = jnp.concatenate([jnp.zeros((seq, half), F32), sin, jnp.zeros((seq, rest), F32)], axis=-1)
    rep = LANES // DIFF_HEAD_DIM
    return jnp.tile(c, (1, rep)), jnp.tile(s_up, (1, rep)), jnp.tile(s_dn, (1, rep))


def kernel(x, norm_ffn1, ffn1_w_gate, ffn1_w_up, ffn1_w_down, norm_mix, w_in, conv_w,
           lambda_q1, lambda_k1, lambda_q2, lambda_k2, subln_w, w_out,
           norm_ffn2, ffn2_w_gate, ffn2_w_up, ffn2_w_down, norm_final):
    b, s, d = x.shape
    depth = norm_ffn1.shape[0]
    attn_w = d // 2
    n_heads = attn_w // HEAD_W
    rope_c, rope_s1, rope_s2 = _rope_lane_tables(s)
    tm_in = min(INPROJ_ROWS, s)
    n_steps = n_heads * (s // _attn_tq(s, tm_in))
    ffn1 = (ffn1_w_gate, ffn1_w_up, ffn1_w_down)
    ffn2 = (ffn2_w_gate, ffn2_w_up, ffn2_w_down)

    def bf_layer(w, l):
        return w[l].astype(BF16)[None]

    outs = []
    for bi in range(b):
        xb = x[bi]
        cur = {"ffn1": [bf_layer(w, 0) for w in ffn1], "w_in": bf_layer(w_in, 0),
               "w_out": bf_layer(w_out, 0)}
        for l in range(depth):
            lam_init = jnp.full((1,), 0.8 - 0.6 * math.exp(-0.3 * l), F32)
            xb = _ffn(xb, norm_ffn1[l], *cur["ffn1"], 0)
            qt, k, vt, conv = _inproj(xb, norm_mix[l], cur["w_in"], 0, rope_c, rope_s1, rope_s2,
                                      conv_w[l], attn_w, tm=tm_in)
            wanted = [(w, l) for w in ffn2]
            if l + 1 < depth:
                wanted += [(w, l + 1) for w in ffn1] + [(w_in, l + 1), (w_out, l + 1)]
            in_kernel = [(w, ll) for w, ll in wanted if _castable(w, n_steps)]
            attn, casted = _attention(qt, k, vt, lam_init, lambda_q1[l], lambda_k1[l],
                                      lambda_q2[l], lambda_k2[l], subln_w[l], n_heads, in_kernel)
            casted = list(casted)
            ready = [casted.pop(0)[None] if _castable(w, n_steps) else bf_layer(w, ll)
                     for w, ll in wanted]
            xb = _out_proj(xb, attn, conv, cur["w_out"], 0)
            g_final = norm_final if l == depth - 1 else None
            xb = _ffn(xb, norm_ffn2[l], *ready[:3], 0, g_final)
            if l + 1 < depth:
                cur = {"ffn1": ready[3:6], "w_in": ready[6], "w_out": ready[7]}
        outs.append(xb)
    return outs[0][None] if b == 1 else jnp.stack(outs, axis=0)
```

```python
import functools
import math

import jax
import jax.numpy as jnp
from jax import lax
from jax.experimental import pallas as pl
from jax.experimental.pallas import tpu as pltpu

F32 = jnp.float32
BF16 = jnp.bfloat16

EPS = 1e-5
NEG_INF = -1e30
ROPE_THETA = 500000.0
DIFF_HEAD_DIM = 64
HEAD_W = 2 * DIFF_HEAD_DIM
ROT_DIM = DIFF_HEAD_DIM // 4
CONV_K = 3
LOG2E = math.log2(math.e)

EXP2_HEADROOM = 60.0
LANES = 128
SUBLANES = 8
VMEM_LIMIT = 56 * 1024 * 1024

FFN_ROWS, FFN_COLS = 1024, 512
NORM_ROWS = 512
INPROJ_ROWS, INPROJ_COLS = 512, 512
OUTPROJ_ROWS = 512
ATTN_Q = 2048
ATTN_UNIT_KEYS, ATTN_UNIT_COLS = 256, 256
ATTN_DEPTH = 4


def _params(*sem):
    return pltpu.CompilerParams(dimension_semantics=sem, vmem_limit_bytes=VMEM_LIMIT)


def _rms(x, g):
    ms = jnp.mean(x * x, axis=-1, keepdims=True)
    return x * lax.rsqrt(ms + EPS) * g


def _ffn_kernel(*refs, final_norm, rows, nf):
    if final_norm:
        x_ref, g_ref, wg_ref, wu_ref, wd_ref, gf_ref, o_ref, h_ref = refs
    else:
        x_ref, g_ref, wg_ref, wu_ref, wd_ref, o_ref, h_ref = refs
    f = pl.program_id(1)
    tm = x_ref.shape[0]

    def block(first, last):
        step = rows if (first or last) else tm
        for r in range(0, tm, step):
            rs = slice(r, r + step)
            if first:
                h = _rms(x_ref[rs, :], g_ref[...]).astype(BF16)
                h_ref[rs, :] = h
            else:
                h = h_ref[rs, :]
            gate = jnp.dot(h, wg_ref[...], preferred_element_type=F32)
            up = jnp.dot(h, wu_ref[...], preferred_element_type=F32)
            act = (gate * (1.0 / (1.0 + jnp.exp(-gate))) * up).astype(BF16)
            o = jnp.dot(act, wd_ref[...], preferred_element_type=F32)
            if not first:
                o = o_ref[rs, :] + o
            if last:
                o = x_ref[rs, :] + 0.5 * o
                if final_norm:
                    o = _rms(o, gf_ref[...])
            o_ref[rs, :] = o

    pl.when(f == 0)(lambda: block(True, nf == 1))
    if nf > 2:
        pl.when(jnp.logical_and(f > 0, f < nf - 1))(lambda: block(False, False))
    if nf > 1:
        pl.when(f == nf - 1)(lambda: block(False, True))


def _ffn(x, g, wg, wu, wd, layer, g_final=None, *, tm=FFN_ROWS, tf=FFN_COLS):
    s, d = x.shape
    dff = wg.shape[2]
    tm, tf = min(tm, s), min(tf, dff)
    assert s % tm == 0 and dff % tf == 0
    final_norm = g_final is not None
    in_specs = [
        pl.BlockSpec((tm, d), lambda i, f: (i, 0)),
        pl.BlockSpec((1, d), lambda i, f: (0, 0)),
        pl.BlockSpec((None, d, tf), lambda i, f: (layer, 0, f)),
        pl.BlockSpec((None, d, tf), lambda i, f: (layer, 0, f)),
        pl.BlockSpec((None, tf, d), lambda i, f: (layer, f, 0)),
    ]
    args = [x, g.reshape(1, d), wg, wu, wd]
    if final_norm:
        in_specs.append(pl.BlockSpec((1, d), lambda i, f: (0, 0)))
        args.append(g_final.reshape(1, d))
    return pl.pallas_call(
        functools.partial(_ffn_kernel, final_norm=final_norm, rows=min(NORM_ROWS, tm),
                          nf=dff // tf),
        grid=(s // tm, dff // tf),
        in_specs=in_specs,
        out_specs=pl.BlockSpec((tm, d), lambda i, f: (i, 0)),
        out_shape=jax.ShapeDtypeStruct((s, d), F32),
        scratch_shapes=[pltpu.VMEM((tm, d), BF16)],
        compiler_params=_params("parallel", "arbitrary"),
        name="ffn_final" if final_norm else "ffn",
    )(*args)


def _inproj_kernel(x_ref, g_ref, w_ref, c_ref, s1_ref, s2_ref, cw_ref,
                   qt_ref, k_ref, vt_ref, conv_ref, h_ref, z_ref, *, q_scale, attn_w, tn):
    i = pl.program_id(0)
    tm = x_ref.shape[0]
    conv_w = conv_ref.shape[1]

    @pl.when(i == 0)
    def _():
        z_ref[0:SUBLANES, :] = jnp.zeros((SUBLANES, conv_w), F32)

    h_ref[...] = _rms(x_ref[...], g_ref[...]).astype(BF16)
    hn = h_ref[...]

    def proj(col):
        return jnp.dot(hn, w_ref[:, col:col + tn], preferred_element_type=F32)

    def rope(t):
        up = pltpu.roll(t, LANES - ROT_DIM // 2, axis=1)
        dn = pltpu.roll(t, ROT_DIM // 2, axis=1)
        return t * c_ref[...] + up * s1_ref[...] + dn * s2_ref[...]

    for n in range(3 * attn_w // tn):
        acc = proj(n * tn)
        kind, base = divmod(n * tn, attn_w)
        for c in range(tn // HEAD_W):
            t = acc[:, c * HEAD_W:(c + 1) * HEAD_W]
            head = base // HEAD_W + c
            if kind == 0:
                qt_ref[head * HEAD_W:(head + 1) * HEAD_W, :] = (rope(t) * q_scale).T.astype(BF16)
            elif kind == 1:
                k_ref[:, head * HEAD_W:(head + 1) * HEAD_W] = rope(t).astype(BF16)
            else:
                vt_ref[head * HEAD_W:(head + 1) * HEAD_W, :] = t.T.astype(BF16)

    for n in range(conv_w // tn):
        cs = slice(n * tn, (n + 1) * tn)
        gate_b = proj(3 * attn_w + n * tn)
        z = proj(3 * attn_w + conv_w + n * tn) * proj(3 * attn_w + 2 * conv_w + n * tn)
        z_ref[SUBLANES:SUBLANES + tm, cs] = z
        z1 = z_ref[SUBLANES - 1:SUBLANES - 1 + tm, cs]
        z2 = z_ref[SUBLANES - 2:SUBLANES - 2 + tm, cs]
        cw = cw_ref[:, cs]
        y = cw[0:1, :] * z2 + cw[1:2, :] * z1 + cw[2:3, :] * z
        conv_ref[:, cs] = (gate_b * y).astype(BF16)
        z_ref[0:SUBLANES, cs] = z_ref[tm:tm + SUBLANES, cs]


def _inproj(x, g, w_in, layer, rope_c, rope_s1, rope_s2, conv_w, attn_w, *, tm, tn=INPROJ_COLS):
    s, d = x.shape
    conv_width = conv_w.shape[1]
    tm = min(tm, s)
    tn = min(tn, attn_w, conv_width)
    assert s % tm == 0 and attn_w % tn == 0 and conv_width % tn == 0
    assert tn % HEAD_W == 0 and HEAD_W == LANES
    assert w_in.shape[2] == 3 * attn_w + 3 * conv_width
    q_scale = DIFF_HEAD_DIM ** -0.5 * LOG2E
    tab = pl.BlockSpec((tm, LANES), lambda i: (i, 0))
    return pl.pallas_call(
        functools.partial(_inproj_kernel, q_scale=q_scale, attn_w=attn_w, tn=tn),
        grid=(s // tm,),
        in_specs=[
            pl.BlockSpec((tm, d), lambda i: (i, 0)),
            pl.BlockSpec((1, d), lambda i: (0, 0)),
            pl.BlockSpec((None, d, w_in.shape[2]), lambda i: (layer, 0, 0),
                         pipeline_mode=pl.Buffered(1)),
            tab, tab, tab,
            pl.BlockSpec((CONV_K, conv_width), lambda i: (0, 0)),
        ],
        out_specs=[
            pl.BlockSpec((attn_w, tm), lambda i: (0, i)),
            pl.BlockSpec((tm, attn_w), lambda i: (i, 0)),
            pl.BlockSpec((None, attn_w, tm), lambda i: (i, 0, 0)),
            pl.BlockSpec((tm, conv_width), lambda i: (i, 0)),
        ],
        out_shape=[
            jax.ShapeDtypeStruct((attn_w, s), BF16),
            jax.ShapeDtypeStruct((s, attn_w), BF16),
            jax.ShapeDtypeStruct((s // tm, attn_w, tm), BF16),
            jax.ShapeDtypeStruct((s, conv_width), BF16),
        ],
        scratch_shapes=[
            pltpu.VMEM((tm, d), BF16),
            pltpu.VMEM((tm + SUBLANES, conv_width), F32),
        ],
        compiler_params=_params("arbitrary"),
        name="in_proj",
    )(x, g.reshape(1, d), w_in, rope_c, rope_s1, rope_s2, conv_w)


def _attn_kernel(*refs, ks, cw, depth, n_cast):
    (li_ref, qt_ref, k_ref, vt_ref, lq1_ref, lk1_ref, lq2_ref, lk2_ref, w_ref) = refs[:9]
    cast_in = refs[9:9 + n_cast]
    o_ref = refs[9 + n_cast]
    cast_out = refs[10 + n_cast:10 + 2 * n_cast]
    qs_ref, m_ref, top_ref, l_ref, acc_ref, st_ref = refs[10 + 2 * n_cast:]
    i = pl.program_id(1)

    cast_pieces = []
    for src, dst in zip(cast_in, cast_out):
        for r in range(0, src.shape[0], 2 * SUBLANES):
            def piece(src=src, dst=dst, r=r):
                dst[r:r + 2 * SUBLANES, :] = src[r:r + 2 * SUBLANES, :].astype(BF16)
            cast_pieces.append(piece)

    tq = qt_ref.shape[1]
    tk = vt_ref.shape[2]

    qt = qt_ref[...]
    row = lax.broadcasted_iota(jnp.int32, qt.shape, 0)
    zero = jnp.zeros_like(qt)
    qs_ref[:, 0:tq] = jnp.where(row < DIFF_HEAD_DIM, qt, zero)
    qs_ref[:, tq:2 * tq] = jnp.where(row >= DIFF_HEAD_DIM, qt, zero)

    nq = tq // tk
    all_units = [(t, c) for t in range(tk // ks) for c in range(2 * tq // cw)]

    def col_pos(c):
        q0 = (c * cw) % tq
        return q0 // tk, q0 % tk

    def diag_units(d):
        keep = []
        for (t, c) in all_units:
            qb, co = col_pos(c)
            if qb > d or (qb == d and t * ks < co + cw):
                keep.append((t, c))
        return keep

    def scores(j, t, c):
        k = k_ref[pl.ds(pl.multiple_of(j * tk + t * ks, ks), ks), :]
        return jnp.dot(k, qs_ref[:, c * cw:(c + 1) * cw], preferred_element_type=F32)

    def masked(t, c, st, diag):
        qb, co = col_pos(c)
        if diag is not None and qb == diag and (t + 1) * ks - 1 > co:
            key = lax.broadcasted_iota(jnp.int32, (ks, cw), 0) + t * ks
            qry = lax.broadcasted_iota(jnp.int32, (ks, cw), 1) + co
            st = jnp.where(key <= qry, st, NEG_INF)
        return st

    def pv(j, t, pt):
        vt = vt_ref[j, :, t * ks:(t + 1) * ks]
        return jnp.dot(vt, pt.astype(BF16), preferred_element_type=F32)

    def unshifted_unit(j, t, c, st, diag):
        cs = slice(c * cw, (c + 1) * cw)
        st = masked(t, c, st, diag)
        top_ref[:, cs] = jnp.maximum(top_ref[:, cs], jnp.max(st, axis=0, keepdims=True))
        pt = jnp.exp2(st)
        l_ref[:, cs] += jnp.sum(pt, axis=0, keepdims=True)
        acc_ref[:, cs] += pv(j, t, pt)

    def running_max_unit(j, t, c, st, diag):
        cs = slice(c * cw, (c + 1) * cw)
        st = masked(t, c, st, diag)
        m_prev = m_ref[:, cs]
        m_new = jnp.maximum(m_prev, jnp.max(st, axis=0, keepdims=True))
        alpha = jnp.exp2(m_prev - m_new)
        pt = jnp.exp2(st - m_new)
        l_ref[:, cs] = alpha * l_ref[:, cs] + jnp.sum(pt, axis=0, keepdims=True)
        acc_ref[:, cs] = alpha * acc_ref[:, cs] + pv(j, t, pt)
        m_ref[:, cs] = m_new

    def at_slab(dj, units):
        return [(dj, t, c) for (t, c) in units]

    def attend(unit_fn, side_jobs=()):
        acc_ref[...] = jnp.zeros(acc_ref.shape, F32)
        l_ref[...] = jnp.zeros(l_ref.shape, F32)

        def step(j, units, diag, next_units, extras=None):
            pending = [st_ref[d] for d in range(depth)]
            for n, (dj, t, c) in enumerate(units):
                for job in (extras[n] if extras else ()):
                    job()
                st = pending.pop(0)
                ahead = n + depth
                if ahead < len(units):
                    aj, at, ac = units[ahead]
                    pending.append(scores(j + aj, at, ac))
                elif next_units is not None:
                    aj, at, ac = next_units[ahead - len(units)]
                    st_ref[ahead - len(units)] = scores(j + aj, at, ac)
                unit_fn(j + dj, t, c, st, diag)

        per = next(p for p in (4, 2, 1) if nq % p == 0)
        loop_units = [u for dj in range(per) for u in at_slab(dj, all_units)]

        def body(jj, carry):
            step(jj * per, loop_units, None, at_slab(per, all_units))
            return carry

        for d in range(depth):
            st_ref[d] = scores(0, *all_units[d])
        lax.fori_loop(0, i * nq // per, body, 0)
        n_diag = sum(len(diag_units(d)) for d in range(nq))
        jobs = [[] for _ in range(n_diag)]
        for n, job in enumerate(side_jobs):
            jobs[n * n_diag // len(side_jobs)].append(job)
        done = 0
        for d in range(nq):
            units = diag_units(d)
            nxt = at_slab(1, diag_units(d + 1)) if d + 1 < nq else None
            step(i * nq + d, at_slab(0, units), d, nxt, jobs[done:done + len(units)])
            done += len(units)

        lam = (jnp.exp(jnp.sum(lq1_ref[...] * lk1_ref[...], axis=-1, keepdims=True))
               - jnp.exp(jnp.sum(lq2_ref[...] * lk2_ref[...], axis=-1, keepdims=True))
               + li_ref[0])
        inv_l = 1.0 / l_ref[...]
        o1 = acc_ref[:, 0:tq] * inv_l[:, 0:tq]
        o2 = acc_ref[:, tq:2 * tq] * inv_l[:, tq:2 * tq]
        ot = o1 - lam * o2
        ms = jnp.mean(ot * ot, axis=0, keepdims=True)
        ot = ot * lax.rsqrt(ms + EPS) * w_ref[...] * (1.0 - li_ref[0])
        o_ref[...] = ot.T.astype(BF16)

    top_ref[...] = jnp.full(top_ref.shape, NEG_INF, F32)
    attend(unshifted_unit, cast_pieces)

    top = top_ref[...]
    in_range = jnp.logical_and(jnp.max(top) <= EXP2_HEADROOM, jnp.min(top) >= -EXP2_HEADROOM)

    @pl.when(jnp.logical_not(in_range))
    def _():
        m_ref[...] = jnp.full(m_ref.shape, NEG_INF, F32)
        attend(running_max_unit)


def _attn_tq(s, tk, tq=ATTN_Q):
    return max(min(tq, s), tk)


def _castable(w, n_steps):
    rows = w.shape[1]
    return rows % n_steps == 0 and (rows // n_steps) % 16 == 0


def _attention(qt, k, vt, lam_init, lq1, lk1, lq2, lk2, subln_w, n_heads, cast=(), *,
               ks=ATTN_UNIT_KEYS, cw=ATTN_UNIT_COLS, depth=ATTN_DEPTH):
    s = k.shape[0]
    tk = vt.shape[2]
    tq, ks, cw = _attn_tq(s, tk), min(ks, tk), min(cw, tk)
    assert s % tq == 0 and tq % tk == 0 and tk % ks == 0
    dh = DIFF_HEAD_DIM
    ni = s // tq
    n_steps = n_heads * ni
    lspec = pl.BlockSpec((1, dh), lambda h, i: (0, 0))
    cast_in_specs, cast_out_specs, cast_out_shapes = [], [], []
    for w, layer in cast:
        rows, cols = w.shape[1] // n_steps, w.shape[2]
        cast_in_specs.append(
            pl.BlockSpec((None, rows, cols), lambda h, i, layer=layer: (layer, h * ni + i, 0)))
        cast_out_specs.append(pl.BlockSpec((rows, cols), lambda h, i: (h * ni + i, 0)))
        cast_out_shapes.append(jax.ShapeDtypeStruct(w.shape[1:], BF16))
    outs = pl.pallas_call(
        functools.partial(_attn_kernel, ks=ks, cw=cw, depth=depth, n_cast=len(cast)),
        grid=(n_heads, ni),
        in_specs=[
            pl.BlockSpec(memory_space=pltpu.SMEM),
            pl.BlockSpec((HEAD_W, tq), lambda h, i: (h, i)),
            pl.BlockSpec((s, HEAD_W), lambda h, i: (0, h)),
            pl.BlockSpec((s // tk, HEAD_W, tk), lambda h, i: (0, h, 0)),
            lspec, lspec, lspec, lspec,
            pl.BlockSpec((HEAD_W, 1), lambda h, i: (0, 0)),
        ] + cast_in_specs,
        out_specs=[pl.BlockSpec((tq, HEAD_W), lambda h, i: (i, h))] + cast_out_specs,
        out_shape=[jax.ShapeDtypeStruct((s, n_heads * HEAD_W), BF16)] + cast_out_shapes,
        scratch_shapes=[
            pltpu.VMEM((HEAD_W, 2 * tq), BF16),
            pltpu.VMEM((1, 2 * tq), F32),
            pltpu.VMEM((1, 2 * tq), F32),
            pltpu.VMEM((1, 2 * tq), F32),
            pltpu.VMEM((HEAD_W, 2 * tq), F32),
            pltpu.VMEM((depth, ks, cw), F32),
        ],
        compiler_params=_params("arbitrary", "arbitrary"),
        name="diff_attn",
    )(lam_init, qt, k, vt, lq1.reshape(1, dh), lk1.reshape(1, dh),
      lq2.reshape(1, dh), lk2.reshape(1, dh), subln_w.reshape(HEAD_W, 1),
      *[w for w, _ in cast])
    return outs[0], outs[1:]


def _out_kernel(x_ref, a_ref, c_ref, wa_ref, wc_ref, o_ref):
    o_ref[...] = (x_ref[...]
                  + jnp.dot(a_ref[...], wa_ref[...], preferred_element_type=F32)
                  + jnp.dot(c_ref[...], wc_ref[...], preferred_element_type=F32))


def _out_proj(x, attn, conv, w_out, layer, *, tm=OUTPROJ_ROWS):
    s, d = x.shape
    aw, cw = attn.shape[1], conv.shape[1]
    tm = min(tm, s)
    assert s % tm == 0 and aw == cw
    return pl.pallas_call(
        _out_kernel,
        grid=(s // tm,),
        in_specs=[
            pl.BlockSpec((tm, d), lambda i: (i, 0)),
            pl.BlockSpec((tm, aw), lambda i: (i, 0)),
            pl.BlockSpec((tm, cw), lambda i: (i, 0)),
            pl.BlockSpec((None, aw, d), lambda i: (layer, 0, 0)),
            pl.BlockSpec((None, cw, d), lambda i: (layer, 1, 0)),
        ],
        out_specs=pl.BlockSpec((tm, d), lambda i: (i, 0)),
        out_shape=jax.ShapeDtypeStruct((s, d), F32),
        compiler_params=_params("parallel"),
        name="out_proj",
    )(x, attn, conv, w_out, w_out)


def _rope_lane_tables(seq):
    half = ROT_DIM // 2
    pos = jnp.arange(seq, dtype=F32)
    inv_freq = ROPE_THETA ** (-jnp.arange(0, ROT_DIM, 2, dtype=F32) / ROT_DIM)
    ang = pos[:, None] * inv_freq[None, :]
    cos, sin = jnp.cos(ang), jnp.sin(ang)
    rest = DIFF_HEAD_DIM - ROT_DIM
    ones = jnp.ones((seq, rest), F32)
    c = jnp.concatenate([cos, cos, ones], axis=-1)
    s_up = jnp.concatenate([-sin, jnp.zeros((seq, half + rest), F32)], axis=-1)
    s_dn = jnp.concatenate([jnp.zeros((seq, half), F32), sin, jnp.zeros((seq, rest), F32)], axis=-1)
    rep = LANES // DIFF_HEAD_DIM
    return jnp.tile(c, (1, rep)), jnp.tile(s_up, (1, rep)), jnp.tile(s_dn, (1, rep))


def kernel(x, norm_ffn1, ffn1_w_gate, ffn1_w_up, ffn1_w_down, norm_mix, w_in, conv_w,
           lambda_q1, lambda_k1, lambda_q2, lambda_k2, subln_w, w_out,
           norm_ffn2, ffn2_w_gate, ffn2_w_up, ffn2_w_down, norm_final):
    b, s, d = x.shape
    depth = norm_ffn1.shape[0]
    attn_w = d // 2
    n_heads = attn_w // HEAD_W
    rope_c, rope_s1, rope_s2 = _rope_lane_tables(s)
    tm_in = min(INPROJ_ROWS, s)
    n_steps = n_heads * (s // _attn_tq(s, tm_in))
    ffn1 = (ffn1_w_gate, ffn1_w_up, ffn1_w_down)
    ffn2 = (ffn2_w_gate, ffn2_w_up, ffn2_w_down)

    def bf_layer(w, l):
        return w[l].astype(BF16)[None]

    outs = []
    for bi in range(b):
        xb = x[bi]
        cur = {"ffn1": [bf_layer(w, 0) for w in ffn1], "w_in": bf_layer(w_in, 0),
               "w_out": bf_layer(w_out, 0)}
        for l in range(depth):
            lam_init = jnp.full((1,), 0.8 - 0.6 * math.exp(-0.3 * l), F32)
            xb = _ffn(xb, norm_ffn1[l], *cur["ffn1"], 0)
            qt, k, vt, conv = _inproj(xb, norm_mix[l], cur["w_in"], 0, rope_c, rope_s1, rope_s2,
                                      conv_w[l], attn_w, tm=tm_in)
            wanted = [(w, l) for w in ffn2]
            if l + 1 < depth:
                wanted += [(w, l + 1) for w in ffn1] + [(w_in, l + 1), (w_out, l + 1)]
            in_kernel = [(w, ll) for w, ll in wanted if _castable(w, n_steps)]
            attn, casted = _attention(qt, k, vt, lam_init, lambda_q1[l], lambda_k1[l],
                                      lambda_q2[l], lambda_k2[l], subln_w[l], n_heads, in_kernel)
            casted = list(casted)
            ready = [casted.pop(0)[None] if _castable(w, n_steps) else bf_layer(w, ll)
                     for w, ll in wanted]
            xb = _out_proj(xb, attn, conv, cur["w_out"], 0)
            g_final = norm_final if l == depth - 1 else None
            xb = _ffn(xb, norm_ffn2[l], *ready[:3], 0, g_final)
            if l + 1 < depth:
                cur = {"ffn1": ready[3:6], "w_in": ready[6], "w_out": ready[7]}
        outs.append(xb)
    return outs[0][None] if b == 1 else jnp.stack(outs, axis=0)
```

```python
import functools
import math

import jax
import jax.numpy as jnp
from jax import lax
from jax.experimental import pallas as pl
from jax.experimental.pallas import tpu as pltpu

F32 = jnp.float32
BF16 = jnp.bfloat16

EPS = 1e-5
NEG_INF = -1e30
ROPE_THETA = 500000.0
DIFF_HEAD_DIM = 64
HEAD_W = 2 * DIFF_HEAD_DIM
ROT_DIM = DIFF_HEAD_DIM // 4
CONV_K = 3
LOG2E = math.log2(math.e)

EXP2_HEADROOM = 60.0
LANES = 128
SUBLANES = 8
VMEM_LIMIT = 56 * 1024 * 1024

FFN_ROWS, FFN_COLS = 1024, 512
NORM_ROWS = 512
INPROJ_ROWS, INPROJ_COLS = 512, 512
OUTPROJ_ROWS = 512
ATTN_Q = 2048
ATTN_UNIT_KEYS, ATTN_UNIT_COLS = 256, 256
ATTN_DEPTH = 4


def _params(*sem):
    return pltpu.CompilerParams(dimension_semantics=sem, vmem_limit_bytes=VMEM_LIMIT)


def _rms(x, g):
    ms = jnp.mean(x * x, axis=-1, keepdims=True)
    return x * lax.rsqrt(ms + EPS) * g


def _ffn_kernel(*refs, final_norm, rows, nf):
    if final_norm:
        x_ref, g_ref, wg_ref, wu_ref, wd_ref, gf_ref, o_ref, h_ref = refs
    else:
        x_ref, g_ref, wg_ref, wu_ref, wd_ref, o_ref, h_ref = refs
    f = pl.program_id(1)
    tm = x_ref.shape[0]

    def block(first, last):
        step = rows if (first or last) else tm
        for r in range(0, tm, step):
            rs = slice(r, r + step)
            if first:
                h = _rms(x_ref[rs, :], g_ref[...]).astype(BF16)
                h_ref[rs, :] = h
            else:
                h = h_ref[rs, :]
            gate = jnp.dot(h, wg_ref[...], preferred_element_type=F32)
            up = jnp.dot(h, wu_ref[...], preferred_element_type=F32)
            act = (gate * (1.0 / (1.0 + jnp.exp(-gate))) * up).astype(BF16)
            o = jnp.dot(act, wd_ref[...], preferred_element_type=F32)
            if not first:
                o = o_ref[rs, :] + o
            if last:
                o = x_ref[rs, :] + 0.5 * o
                if final_norm:
                    o = _rms(o, gf_ref[...])
            o_ref[rs, :] = o

    pl.when(f == 0)(lambda: block(True, nf == 1))
    if nf > 2:
        pl.when(jnp.logical_and(f > 0, f < nf - 1))(lambda: block(False, False))
    if nf > 1:
        pl.when(f == nf - 1)(lambda: block(False, True))


def _ffn(x, g, wg, wu, wd, layer, g_final=None, *, tm=FFN_ROWS, tf=FFN_COLS):
    s, d = x.shape
    dff = wg.shape[2]
    tm, tf = min(tm, s), min(tf, dff)
    assert s % tm == 0 and dff % tf == 0
    final_norm = g_final is not None
    in_specs = [
        pl.BlockSpec((tm, d), lambda i, f: (i, 0)),
        pl.BlockSpec((1, d), lambda i, f: (0, 0)),
        pl.BlockSpec((None, d, tf), lambda i, f: (layer, 0, f)),
        pl.BlockSpec((None, d, tf), lambda i, f: (layer, 0, f)),
        pl.BlockSpec((None, tf, d), lambda i, f: (layer, f, 0)),
    ]
    args = [x, g.reshape(1, d), wg, wu, wd]
    if final_norm:
        in_specs.append(pl.BlockSpec((1, d), lambda i, f: (0, 0)))
        args.append(g_final.reshape(1, d))
    return pl.pallas_call(
        functools.partial(_ffn_kernel, final_norm=final_norm, rows=min(NORM_ROWS, tm),
                          nf=dff // tf),
        grid=(s // tm, dff // tf),
        in_specs=in_specs,
        out_specs=pl.BlockSpec((tm, d), lambda i, f: (i, 0)),
        out_shape=jax.ShapeDtypeStruct((s, d), F32),
        scratch_shapes=[pltpu.VMEM((tm, d), BF16)],
        compiler_params=_params("parallel", "arbitrary"),
        name="ffn_final" if final_norm else "ffn",
    )(*args)


def _inproj_kernel(x_ref, g_ref, w_ref, c_ref, s1_ref, s2_ref, cw_ref,
                   qt_ref, k_ref, vt_ref, conv_ref, h_ref, z_ref, *, q_scale, attn_w, tn):
    i = pl.program_id(0)
    tm = x_ref.shape[0]
    conv_w = conv_ref.shape[1]

    @pl.when(i == 0)
    def _():
        z_ref[0:SUBLANES, :] = jnp.zeros((SUBLANES, conv_w), F32)

    h_ref[...] = _rms(x_ref[...], g_ref[...]).astype(BF16)
    hn = h_ref[...]

    def proj(col):
        return jnp.dot(hn, w_ref[:, col:col + tn], preferred_element_type=F32)

    def rope(t):
        up = pltpu.roll(t, LANES - ROT_DIM // 2, axis=1)
        dn = pltpu.roll(t, ROT_DIM // 2, axis=1)
        return t * c_ref[...] + up * s1_ref[...] + dn * s2_ref[...]

    for n in range(3 * attn_w // tn):
        acc = proj(n * tn)
        kind, base = divmod(n * tn, attn_w)
        for c in range(tn // HEAD_W):
            t = acc[:, c * HEAD_W:(c + 1) * HEAD_W]
            head = base // HEAD_W + c
            if kind == 0:
                qt_ref[head * HEAD_W:(head + 1) * HEAD_W, :] = (rope(t) * q_scale).T.astype(BF16)
            elif kind == 1:
                k_ref[:, head * HEAD_W:(head + 1) * HEAD_W] = rope(t).astype(BF16)
            else:
                vt_ref[head * HEAD_W:(head + 1) * HEAD_W, :] = t.T.astype(BF16)

    for n in range(conv_w // tn):
        cs = slice(n * tn, (n + 1) * tn)
        gate_b = proj(3 * attn_w + n * tn)
        z = proj(3 * attn_w + conv_w + n * tn) * proj(3 * attn_w + 2 * conv_w + n * tn)
        z_ref[SUBLANES:SUBLANES + tm, cs] = z
        z1 = z_ref[SUBLANES - 1:SUBLANES - 1 + tm, cs]
        z2 = z_ref[SUBLANES - 2:SUBLANES - 2 + tm, cs]
        cw = cw_ref[:, cs]
        y = cw[0:1, :] * z2 + cw[1:2, :] * z1 + cw[2:3, :] * z
        conv_ref[:, cs] = (gate_b * y).astype(BF16)
        z_ref[0:SUBLANES, cs] = z_ref[tm:tm + SUBLANES, cs]


def _inproj(x, g, w_in, layer, rope_c, rope_s1, rope_s2, conv_w, attn_w, *, tm, tn=INPROJ_COLS):
    s, d = x.shape
    conv_width = conv_w.shape[1]
    tm = min(tm, s)
    tn = min(tn, attn_w, conv_width)
    assert s % tm == 0 and attn_w % tn == 0 and conv_width % tn == 0
    assert tn % HEAD_W == 0 and HEAD_W == LANES
    assert w_in.shape[2] == 3 * attn_w + 3 * conv_width
    q_scale = DIFF_HEAD_DIM ** -0.5 * LOG2E
    tab = pl.BlockSpec((tm, LANES), lambda i: (i, 0))
    return pl.pallas_call(
        functools.partial(_inproj_kernel, q_scale=q_scale, attn_w=attn_w, tn=tn),
        grid=(s // tm,),
        in_specs=[
            pl.BlockSpec((tm, d), lambda i: (i, 0)),
            pl.BlockSpec((1, d), lambda i: (0, 0)),
            pl.BlockSpec((None, d, w_in.shape[2]), lambda i: (layer, 0, 0),
                         pipeline_mode=pl.Buffered(1)),
            tab, tab, tab,
            pl.BlockSpec((CONV_K, conv_width), lambda i: (0, 0)),
        ],
        out_specs=[
            pl.BlockSpec((attn_w, tm), lambda i: (0, i)),
            pl.BlockSpec((tm, attn_w), lambda i: (i, 0)),
            pl.BlockSpec((None, attn_w, tm), lambda i: (i, 0, 0)),
            pl.BlockSpec((tm, conv_width), lambda i: (i, 0)),
        ],
        out_shape=[
            jax.ShapeDtypeStruct((attn_w, s), BF16),
            jax.ShapeDtypeStruct((s, attn_w), BF16),
            jax.ShapeDtypeStruct((s // tm, attn_w, tm), BF16),
            jax.ShapeDtypeStruct((s, conv_width), BF16),
        ],
        scratch_shapes=[
            pltpu.VMEM((tm, d), BF16),
            pltpu.VMEM((tm + SUBLANES, conv_width), F32),
        ],
        compiler_params=_params("arbitrary"),
        name="in_proj",
    )(x, g.reshape(1, d), w_in, rope_c, rope_s1, rope_s2, conv_w)


def _attn_kernel(*refs, ks, cw, depth, n_cast):
    (li_ref, qt_ref, k_ref, vt_ref, lq1_ref, lk1_ref, lq2_ref, lk2_ref, w_ref) = refs[:9]
    cast_in = refs[9:9 + n_cast]
    o_ref = refs[9 + n_cast]
    cast_out = refs[10 + n_cast:10 + 2 * n_cast]
    qs_ref, m_ref, top_ref, l_ref, acc_ref, st_ref = refs[10 + 2 * n_cast:]
    i = pl.program_id(1)

    cast_pieces = []
    for src, dst in zip(cast_in, cast_out):
        for r in range(0, src.shape[0], 2 * SUBLANES):
            def piece(src=src, dst=dst, r=r):
                dst[r:r + 2 * SUBLANES, :] = src[r:r + 2 * SUBLANES, :].astype(BF16)
            cast_pieces.append(piece)

    tq = qt_ref.shape[1]
    tk = vt_ref.shape[2]

    qt = qt_ref[...]
    row = lax.broadcasted_iota(jnp.int32, qt.shape, 0)
    zero = jnp.zeros_like(qt)
    qs_ref[:, 0:tq] = jnp.where(row < DIFF_HEAD_DIM, qt, zero)
    qs_ref[:, tq:2 * tq] = jnp.where(row >= DIFF_HEAD_DIM, qt, zero)

    nq = tq // tk
    all_units = [(t, c) for t in range(tk // ks) for c in range(2 * tq // cw)]

    def col_pos(c):
        q0 = (c * cw) % tq
        return q0 // tk, q0 % tk

    def diag_units(d):
        keep = []
        for (t, c) in all_units:
            qb, co = col_pos(c)
            if qb > d or (qb == d and t * ks < co + cw):
                keep.append((t, c))
        return keep

    def scores(j, t, c):
        k = k_ref[pl.ds(pl.multiple_of(j * tk + t * ks, ks), ks), :]
        return jnp.dot(k, qs_ref[:, c * cw:(c + 1) * cw], preferred_element_type=F32)

    def masked(t, c, st, diag):
        qb, co = col_pos(c)
        if diag is not None and qb == diag and (t + 1) * ks - 1 > co:
            key = lax.broadcasted_iota(jnp.int32, (ks, cw), 0) + t * ks
            qry = lax.broadcasted_iota(jnp.int32, (ks, cw), 1) + co
            st = jnp.where(key <= qry, st, NEG_INF)
        return st

    def pv(j, t, pt):
        vt = vt_ref[j, :, t * ks:(t + 1) * ks]
        return jnp.dot(vt, pt.astype(BF16), preferred_element_type=F32)

    def unshifted_unit(j, t, c, st, diag):
        cs = slice(c * cw, (c + 1) * cw)
        st = masked(t, c, st, diag)
        top_ref[:, cs] = jnp.maximum(top_ref[:, cs], jnp.max(st, axis=0, keepdims=True))
        pt = jnp.exp2(st)
        l_ref[:, cs] += jnp.sum(pt, axis=0, keepdims=True)
        acc_ref[:, cs] += pv(j, t, pt)

    def running_max_unit(j, t, c, st, diag):
        cs = slice(c * cw, (c + 1) * cw)
        st = masked(t, c, st, diag)
        m_prev = m_ref[:, cs]
        m_new = jnp.maximum(m_prev, jnp.max(st, axis=0, keepdims=True))
        alpha = jnp.exp2(m_prev - m_new)
        pt = jnp.exp2(st - m_new)
        l_ref[:, cs] = alpha * l_ref[:, cs] + jnp.sum(pt, axis=0, keepdims=True)
        acc_ref[:, cs] = alpha * acc_ref[:, cs] + pv(j, t, pt)
        m_ref[:, cs] = m_new

    def at_slab(dj, units):
        return [(dj, t, c) for (t, c) in units]

    def attend(unit_fn, side_jobs=()):
        acc_ref[...] = jnp.zeros(acc_ref.shape, F32)
        l_ref[...] = jnp.zeros(l_ref.shape, F32)

        def step(j, units, diag, next_units, extras=None):
            pending = [st_ref[d] for d in range(depth)]
            for n, (dj, t, c) in enumerate(units):
                for job in (extras[n] if extras else ()):
                    job()
                st = pending.pop(0)
                ahead = n + depth
                if ahead < len(units):
                    aj, at, ac = units[ahead]
                    pending.append(scores(j + aj, at, ac))
                elif next_units is not None:
                    aj, at, ac = next_units[ahead - len(units)]
                    st_ref[ahead - len(units)] = scores(j + aj, at, ac)
                unit_fn(j + dj, t, c, st, diag)

        per = 2 if nq % 2 == 0 else 1
        loop_units = [u for dj in range(per) for u in at_slab(dj, all_units)]

        def body(jj, carry):
            step(jj * per, loop_units, None, at_slab(per, all_units))
            return carry

        for d in range(depth):
            st_ref[d] = scores(0, *all_units[d])
        lax.fori_loop(0, i * nq // per, body, 0)
        n_diag = sum(len(diag_units(d)) for d in range(nq))
        jobs = [[] for _ in range(n_diag)]
        for n, job in enumerate(side_jobs):
            jobs[n * n_diag // len(side_jobs)].append(job)
        done = 0
        for d in range(nq):
            units = diag_units(d)
            nxt = at_slab(1, diag_units(d + 1)) if d + 1 < nq else None
            step(i * nq + d, at_slab(0, units), d, nxt, jobs[done:done + len(units)])
            done += len(units)

        lam = (jnp.exp(jnp.sum(lq1_ref[...] * lk1_ref[...], axis=-1, keepdims=True))
               - jnp.exp(jnp.sum(lq2_ref[...] * lk2_ref[...], axis=-1, keepdims=True))
               + li_ref[0])
        inv_l = 1.0 / l_ref[...]
        o1 = acc_ref[:, 0:tq] * inv_l[:, 0:tq]
        o2 = acc_ref[:, tq:2 * tq] * inv_l[:, tq:2 * tq]
        ot = o1 - lam * o2
        ms = jnp.mean(ot * ot, axis=0, keepdims=True)
        ot = ot * lax.rsqrt(ms + EPS) * w_ref[...] * (1.0 - li_ref[0])
        o_ref[...] = ot.astype(BF16)

    top_ref[...] = jnp.full(top_ref.shape, NEG_INF, F32)
    attend(unshifted_unit, cast_pieces)

    top = top_ref[...]
    in_range = jnp.logical_and(jnp.max(top) <= EXP2_HEADROOM, jnp.min(top) >= -EXP2_HEADROOM)

    @pl.when(jnp.logical_not(in_range))
    def _():
        m_ref[...] = jnp.full(m_ref.shape, NEG_INF, F32)
        attend(running_max_unit)


def _attn_tq(s, tk, tq=ATTN_Q):
    return max(min(tq, s), tk)


def _castable(w, n_steps):
    rows = w.shape[1]
    return rows % n_steps == 0 and (rows // n_steps) % 16 == 0


def _attention(qt, k, vt, lam_init, lq1, lk1, lq2, lk2, subln_w, n_heads, cast=(), *,
               ks=ATTN_UNIT_KEYS, cw=ATTN_UNIT_COLS, depth=ATTN_DEPTH):
    s = k.shape[0]
    tk = vt.shape[2]
    tq, ks, cw = _attn_tq(s, tk), min(ks, tk), min(cw, tk)
    assert s % tq == 0 and tq % tk == 0 and tk % ks == 0
    dh = DIFF_HEAD_DIM
    ni = s // tq
    n_steps = n_heads * ni
    lspec = pl.BlockSpec((1, dh), lambda h, i: (0, 0))
    cast_in_specs, cast_out_specs, cast_out_shapes = [], [], []
    for w, layer in cast:
        rows, cols = w.shape[1] // n_steps, w.shape[2]
        cast_in_specs.append(
            pl.BlockSpec((None, rows, cols), lambda h, i, layer=layer: (layer, h * ni + i, 0)))
        cast_out_specs.append(pl.BlockSpec((rows, cols), lambda h, i: (h * ni + i, 0)))
        cast_out_shapes.append(jax.ShapeDtypeStruct(w.shape[1:], BF16))
    outs = pl.pallas_call(
        functools.partial(_attn_kernel, ks=ks, cw=cw, depth=depth, n_cast=len(cast)),
        grid=(n_heads, ni),
        in_specs=[
            pl.BlockSpec(memory_space=pltpu.SMEM),
            pl.BlockSpec((HEAD_W, tq), lambda h, i: (h, i)),
            pl.BlockSpec((s, HEAD_W), lambda h, i: (0, h)),
            pl.BlockSpec((s // tk, HEAD_W, tk), lambda h, i: (0, h, 0)),
            lspec, lspec, lspec, lspec,
            pl.BlockSpec((HEAD_W, 1), lambda h, i: (0, 0)),
        ] + cast_in_specs,
        out_specs=[pl.BlockSpec((HEAD_W, tq), lambda h, i: (h, i))] + cast_out_specs,
        out_shape=[jax.ShapeDtypeStruct((n_heads * HEAD_W, s), BF16)] + cast_out_shapes,
        scratch_shapes=[
            pltpu.VMEM((HEAD_W, 2 * tq), BF16),
            pltpu.VMEM((1, 2 * tq), F32),
            pltpu.VMEM((1, 2 * tq), F32),
            pltpu.VMEM((1, 2 * tq), F32),
            pltpu.VMEM((HEAD_W, 2 * tq), F32),
            pltpu.VMEM((depth, ks, cw), F32),
        ],
        compiler_params=_params("arbitrary", "arbitrary"),
        name="diff_attn",
    )(lam_init, qt, k, vt, lq1.reshape(1, dh), lk1.reshape(1, dh),
      lq2.reshape(1, dh), lk2.reshape(1, dh), subln_w.reshape(HEAD_W, 1),
      *[w for w, _ in cast])
    return outs[0], outs[1:]


def _out_kernel(x_ref, a_ref, c_ref, wa_ref, wc_ref, o_ref):
    attn = lax.dot_general(a_ref[...], wa_ref[...], (((0,), (0,)), ((), ())),
                           preferred_element_type=F32)
    o_ref[...] = (x_ref[...] + attn
                  + jnp.dot(c_ref[...], wc_ref[...], preferred_element_type=F32))


def _out_proj(x, attn, conv, w_out, layer, *, tm=OUTPROJ_ROWS):
    s, d = x.shape
    aw, cw = attn.shape[0], conv.shape[1]
    tm = min(tm, s)
    assert s % tm == 0 and aw == cw
    return pl.pallas_call(
        _out_kernel,
        grid=(s // tm,),
        in_specs=[
            pl.BlockSpec((tm, d), lambda i: (i, 0)),
            pl.BlockSpec((aw, tm), lambda i: (0, i)),
            pl.BlockSpec((tm, cw), lambda i: (i, 0)),
            pl.BlockSpec((None, aw, d), lambda i: (layer, 0, 0)),
            pl.BlockSpec((None, cw, d), lambda i: (layer, 1, 0)),
        ],
        out_specs=pl.BlockSpec((tm, d), lambda i: (i, 0)),
        out_shape=jax.ShapeDtypeStruct((s, d), F32),
        compiler_params=_params("parallel"),
        name="out_proj",
    )(x, attn, conv, w_out, w_out)


def _rope_lane_tables(seq):
    half = ROT_DIM // 2
    pos = jnp.arange(seq, dtype=F32)
    inv_freq = ROPE_THETA ** (-jnp.arange(0, ROT_DIM, 2, dtype=F32) / ROT_DIM)
    ang = pos[:, None] * inv_freq[None, :]
    cos, sin = jnp.cos(ang), jnp.sin(ang)
    rest = DIFF_HEAD_DIM - ROT_DIM
    ones = jnp.ones((seq, rest), F32)
    c = jnp.concatenate([cos, cos, ones], axis=-1)
    s_up = jnp.concatenate([-sin, jnp.zeros((seq, half + rest), F32)], axis=-1)
    s_dn = jnp.concatenate([jnp.zeros((seq, half), F32), sin, jnp.zeros((seq, rest), F32)], axis=-1)
    rep = LANES // DIFF_HEAD_DIM
    return jnp.tile(c, (1, rep)), jnp.tile(s_up, (1, rep)), jnp.tile(s_dn, (1, rep))


def kernel(x, norm_ffn1, ffn1_w_gate, ffn1_w_up, ffn1_w_down, norm_mix, w_in, conv_w,
           lambda_q1, lambda_k1, lambda_q2, lambda_k2, subln_w, w_out,
           norm_ffn2, ffn2_w_gate, ffn2_w_up, ffn2_w_down, norm_final):
    b, s, d = x.shape
    depth = norm_ffn1.shape[0]
    attn_w = d // 2
    n_heads = attn_w // HEAD_W
    rope_c, rope_s1, rope_s2 = _rope_lane_tables(s)
    tm_in = min(INPROJ_ROWS, s)
    n_steps = n_heads * (s // _attn_tq(s, tm_in))
    ffn1 = (ffn1_w_gate, ffn1_w_up, ffn1_w_down)
    ffn2 = (ffn2_w_gate, ffn2_w_up, ffn2_w_down)

    def bf_layer(w, l):
        return w[l].astype(BF16)[None]

    outs = []
    for bi in range(b):
        xb = x[bi]
        cur = {"ffn1": [bf_layer(w, 0) for w in ffn1], "w_in": bf_layer(w_in, 0),
               "w_out": bf_layer(w_out, 0)}
        for l in range(depth):
            lam_init = jnp.full((1,), 0.8 - 0.6 * math.exp(-0.3 * l), F32)
            xb = _ffn(xb, norm_ffn1[l], *cur["ffn1"], 0)
            qt, k, vt, conv = _inproj(xb, norm_mix[l], cur["w_in"], 0, rope_c, rope_s1, rope_s2,
                                      conv_w[l], attn_w, tm=tm_in)
            wanted = [(w, l) for w in ffn2]
            if l + 1 < depth:
                wanted += [(w, l + 1) for w in ffn1] + [(w_in, l + 1), (w_out, l + 1)]
            in_kernel = [(w, ll) for w, ll in wanted if _castable(w, n_steps)]
            attn, casted = _attention(qt, k, vt, lam_init, lambda_q1[l], lambda_k1[l],
                                      lambda_q2[l], lambda_k2[l], subln_w[l], n_heads, in_kernel)
            casted = list(casted)
            ready = [casted.pop(0)[None] if _castable(w, n_steps) else bf_layer(w, ll)
                     for w, ll in wanted]
            xb = _out_proj(xb, attn, conv, cur["w_out"], 0)
            g_final = norm_final if l == depth - 1 else None
            xb = _ffn(xb, norm_ffn2[l], *ready[:3], 0, g_final)
            if l + 1 < depth:
                cur = {"ffn1": ready[3:6], "w_in": ready[6], "w_out": ready[7]}
        outs.append(xb)
    return outs[0][None] if b == 1 else jnp.stack(outs, axis=0)
```
